```python
import jax, jax.numpy as jnp
from jax import lax
import numpy as np

D_MODEL = 2048
BATCH = 2
SEQ = 8192
DEPTH = 4

CTX_LEN = 256
GRID_W = 64
RWKV_WIDTH = 1024
RWKV_HEAD = 64
RWKV_HEADS = RWKV_WIDTH // RWKV_HEAD
DECAY_LORA = 64
AAA_LORA = 64
GATE_LORA = 128
GN_EPS = 64e-5
CONV_WIDTH = 1024
CONV_K = 31
LN_EPS = 1e-5
N_EXPERTS = 16
EXPERT_FF = 1024
EC_CAPACITY = 2
N_MOD = 6
RMS_EPS = 1e-6

SHIFT_COLS = 3 * RWKV_WIDTH + DECAY_LORA + AAA_LORA
OFF_WA_F = 3 * RWKV_WIDTH
OFF_WA_B = OFF_WA_F + DECAY_LORA + AAA_LORA
OFF_GLAT = OFF_WA_B + DECAY_LORA + AAA_LORA
RWKV_SCAN_COLS = OFF_GLAT
OFF_GLU = OFF_GLAT + GATE_LORA
OFF_GATE = OFF_GLU + 2 * CONV_WIDTH
IN_COLS = OFF_GATE + 2 * D_MODEL

kernel_name = 'hybrid_rwkv7_conformer_ecmoe_dit'


def _rmsnorm(x, g):
    xf = x.astype(jnp.float32)
    y = xf * lax.rsqrt(jnp.mean(xf * xf, axis=-1, keepdims=True) + RMS_EPS)
    return (y * g.astype(jnp.float32)).astype(x.dtype)


def _modulate(h, shift, scale):
    return h * (1 + scale) + shift


def _heads(t):
    return t.reshape(t.shape[:-1] + (RWKV_HEADS, RWKV_HEAD))


def _token_shift(z, direction):
    if direction > 0:
        return jnp.pad(z[:, :-1], ((0, 0), (1, 0), (0, 0)))
    return jnp.pad(z[:, 1:], ((0, 0), (0, 1), (0, 0)))


def _rwkv_direction(P, lp, d):
    direction = 1 if d == 0 else -1
    off = OFF_WA_F if d == 0 else OFF_WA_B
    z = jnp.concatenate([P[..., :3 * RWKV_WIDTH], P[..., off:off + DECAY_LORA + AAA_LORA]], axis=-1)
    z = z + lp['tok_mu'][d] * (_token_shift(z, direction) - z)
    r, k, v, zw, za = jnp.split(z, [RWKV_WIDTH, 2 * RWKV_WIDTH, 3 * RWKV_WIDTH, 3 * RWKV_WIDTH + DECAY_LORA], axis=-1)
    w_log = -jax.nn.softplus(-(lp['w0'][d] + jnp.tanh(zw) @ lp['w2'][d])) - 0.5
    decay = jnp.exp(-jnp.exp(w_log.astype(jnp.float32)))
    a = jax.nn.sigmoid(lp['a0'][d] + za @ lp['a2'][d])
    kk = _heads((k * lp['k_k'][d]).astype(jnp.float32))
    kk = kk / jnp.maximum(jnp.sqrt(jnp.sum(kk * kk, axis=-1, keepdims=True)), 1e-12)
    k = k * (1 + (a - 1) * lp['k_a'][d])
    return _heads(r), _heads(k), _heads(v), _heads(decay), kk, _heads(a)


def _delta_scan(r, k, v, decay, kk, a, s0, reverse, emit=True):
    xs = tuple(jnp.moveaxis(t.astype(jnp.float32), 1, 0) for t in (r, k, v, decay, kk, a))

    def step(S, inp):
        r_t, k_t, v_t, w_t, kk_t, a_t = inp
        sa = jnp.einsum('bhvk,bhk->bhv', S, kk_t)
        S = (S * w_t[:, :, None, :] - sa[..., None] * (kk_t * a_t)[:, :, None, :]
             + v_t[..., None] * k_t[:, :, None, :])
        y = jnp.einsum('bhvk,bhk->bhv', S, r_t) if emit else None
        return S, y

    s_T, ys = lax.scan(step, s0, xs, reverse=reverse)
    return (jnp.moveaxis(ys, 0, 1) if emit else None), s_T


def _rwkv_branch(P, lp, s0_f, s0_b):
    outs = []
    for d, s0 in ((0, s0_f), (1, s0_b)):
        r, k, v, w, kk, a = _rwkv_direction(P, lp, d)
        y, s_T = _delta_scan(r, k, v, w, kk, a, s0, reverse=(d == 1))
        bonus = jnp.sum(r * k * _heads(lp['r_k']), axis=-1, keepdims=True) * v
        outs.append((y, bonus, s_T))
    (y_f, bonus_f, sT_f), (y_b, bonus_b, sT_b) = outs
    y = y_f + y_b
    mu = jnp.mean(y, axis=-1, keepdims=True)
    var = jnp.mean(jnp.square(y - mu), axis=-1, keepdims=True)
    y = (y - mu) * lax.rsqrt(var + GN_EPS) * _heads(lp['lnx_g']) + _heads(lp['lnx_b'])
    y = y.astype(P.dtype) + bonus_f + bonus_b
    B, L = P.shape[0], P.shape[1]
    g = jax.nn.sigmoid(P[..., OFF_GLAT:OFF_GLU]) @ lp['g2']
    return (y.reshape(B, L, RWKV_WIDTH) * g) @ lp['w_rwkv_out'], sT_f, sT_b


def _conformer_branch(P, lp, grid):
    glu = P[..., OFF_GLU:OFF_GATE]
    u = glu[..., :CONV_WIDTH] * jax.nn.sigmoid(glu[..., CONV_WIDTH:])
    B, L, C = u.shape
    if grid:
        rows = L // GRID_W
        seqs = u.reshape(B * rows, GRID_W, C)
    else:
        seqs = u
    pad = CONV_K // 2
    seqs = lax.conv_general_dilated(seqs, lp['conv_w'][:, None, :], window_strides=(1,),
                                    padding=[(pad, pad)], dimension_numbers=('NWC', 'WIO', 'NWC'),
                                    feature_group_count=C)
    u = seqs.reshape(B, L, C) + lp['conv_b']
    uf = u.astype(jnp.float32)
    mu = jnp.mean(uf, axis=-1, keepdims=True)
    var = jnp.mean(jnp.square(uf - mu), axis=-1, keepdims=True)
    un = (uf - mu) * lax.rsqrt(var + LN_EPS) * lp['conv_ln_g'] + lp['conv_ln_b']
    return jax.nn.silu(un).astype(P.dtype) @ lp['w_conv_out']


def _mixer_sublayer(x, shift, scale, gate, lp, s0_f, s0_b, grid):
    h = _modulate(_rmsnorm(x, lp['norm1_g']), shift, scale)
    P = h @ lp['w_in']
    b_r, sT_f, sT_b = _rwkv_branch(P, lp, s0_f, s0_b)
    b_c = _conformer_branch(P, lp, grid)
    g_r = jax.nn.sigmoid(P[..., OFF_GATE:OFF_GATE + D_MODEL])
    g_c = jax.nn.sigmoid(P[..., OFF_GATE + D_MODEL:])
    m = (g_r * b_r + g_c * b_c) @ lp['w_o']
    return x + gate * m, sT_f, sT_b


def _context_states(ctx, shift, scale, lp, s0):
    h = _modulate(_rmsnorm(ctx, lp['norm1_g']), shift, scale)
    P = h @ lp['w_in'][:, :RWKV_SCAN_COLS]
    states = []
    for d in range(2):
        r, k, v, w, kk, a = _rwkv_direction(P, lp, d)
        _, s_T = _delta_scan(r, k, v, w, kk, a, s0, reverse=(d == 1), emit=False)
        states.append(s_T)
    return states[0], states[1]


def _ec_moe(h, lp):
    B, L, D = h.shape
    cap = EC_CAPACITY * L // N_EXPERTS
    aff = jax.nn.softmax((h @ lp['w_router']).astype(jnp.float32), axis=-1)
    vals, idx = lax.top_k(jnp.swapaxes(aff, 1, 2), cap)
    xs = jax.vmap(lambda hb, ib: hb[ib])(h, idx)
    gh = jnp.einsum('becd,edf->becf', xs, lp['w_exp_gate'])
    uh = jnp.einsum('becd,edf->becf', xs, lp['w_exp_up'])
    y = jnp.einsum('becf,efd->becd', jax.nn.silu(gh) * uh, lp['w_exp_down'])
    y = y * vals[..., None].astype(h.dtype)
    return jax.vmap(lambda ib, yb: jnp.zeros((L, D), h.dtype).at[ib.reshape(-1)].add(yb.reshape(-1, D)))(idx, y)


def setup_inputs(seed: int = 0) -> dict:
    key = jax.random.key(seed)
    ks = iter(jax.random.split(key, 48))

    def nrm(shape, scale):
        return scale * jax.random.normal(next(ks), shape, jnp.float32)

    NL, D, R, CC = DEPTH, D_MODEL, RWKV_WIDTH, CONV_WIDTH
    return {
        'x': nrm((BATCH, SEQ, D), 1.0),
        'c': nrm((BATCH, D), 1.0),
        'ctx': nrm((BATCH, CTX_LEN, D), 1.0),
        'c_ctx': nrm((D,), 1.0),
        'w_mod': nrm((NL, D, N_MOD * D), 0.5 * D ** -0.5),
        'b_mod': nrm((NL, N_MOD * D), 0.02),
        'norm1_g': 1.0 + nrm((NL, D), 0.02),
        'w_in': nrm((NL, D, IN_COLS), D ** -0.5),
        'tok_mu': jax.random.uniform(next(ks), (NL, 2, SHIFT_COLS), jnp.float32),
        'w0': -2.5 + nrm((NL, 2, R), 1.5),
        'w2': nrm((NL, 2, DECAY_LORA, R), DECAY_LORA ** -0.5),
        'a0': nrm((NL, 2, R), 0.5),
        'a2': nrm((NL, 2, AAA_LORA, R), AAA_LORA ** -0.5),
        'k_k': 0.85 + nrm((NL, 2, R), 0.05),
        'k_a': 1.0 + nrm((NL, 2, R), 0.05),
        'r_k': nrm((NL, R), 0.1),
        'g2': nrm((NL, GATE_LORA, R), GATE_LORA ** -0.5),
        'lnx_g': 1.0 + nrm((NL, R), 0.02),
        'lnx_b': nrm((NL, R), 0.02),
        'w_rwkv_out': nrm((NL, R, D), R ** -0.5),
        'conv_w': nrm((NL, CONV_K, CC), CONV_K ** -0.5),
        'conv_b': nrm((NL, CC), 0.02),
        'conv_ln_g': 1.0 + nrm((NL, CC), 0.02),
        'conv_ln_b': nrm((NL, CC), 0.02),
        'w_conv_out': nrm((NL, CC, D), CC ** -0.5),
        'w_o': nrm((NL, D, D), D ** -0.5),
        'norm2_g': 1.0 + nrm((NL, D), 0.02),
        'w_router': nrm((NL, D, N_EXPERTS), D ** -0.5),
        'w_exp_gate': nrm((NL, N_EXPERTS, D, EXPERT_FF), D ** -0.5),
        'w_exp_up': nrm((NL, N_EXPERTS, D, EXPERT_FF), D ** -0.5),
        'w_exp_down': nrm((NL, N_EXPERTS, EXPERT_FF, D), EXPERT_FF ** -0.5),
        'final_g': 1.0 + nrm((D,), 0.02),
    }


def reference(x, c, ctx, c_ctx, w_mod, b_mod, norm1_g, w_in, tok_mu, w0, w2, a0, a2, k_k, k_a, r_k,
              g2, lnx_g, lnx_b, w_rwkv_out, conv_w, conv_b, conv_ln_g, conv_ln_b, w_conv_out, w_o,
              norm2_g, w_router, w_exp_gate, w_exp_up, w_exp_down, final_g):
    batch = x.shape[0]
    zero_state = jnp.zeros((batch, RWKV_HEADS, RWKV_HEAD, RWKV_HEAD), jnp.float32)
    for l in range(DEPTH):
        lp = dict(norm1_g=norm1_g[l], w_in=w_in[l], tok_mu=tok_mu[l], w0=w0[l], w2=w2[l], a0=a0[l],
                  a2=a2[l], k_k=k_k[l], k_a=k_a[l], r_k=r_k[l], g2=g2[l], lnx_g=lnx_g[l], lnx_b=lnx_b[l],
                  w_rwkv_out=w_rwkv_out[l], conv_w=conv_w[l], conv_b=conv_b[l], conv_ln_g=conv_ln_g[l],
                  conv_ln_b=conv_ln_b[l], w_conv_out=w_conv_out[l], w_o=w_o[l], norm2_g=norm2_g[l],
                  w_router=w_router[l], w_exp_gate=w_exp_gate[l], w_exp_up=w_exp_up[l],
                  w_exp_down=w_exp_down[l])
        m_lat = jnp.split((jax.nn.silu(c) @ w_mod[l] + b_mod[l])[:, None, :], N_MOD, axis=-1)
        m_ctx = jnp.split((jax.nn.silu(c_ctx) @ w_mod[l] + b_mod[l])[None, None, :], N_MOD, axis=-1)
        last = l == DEPTH - 1
        if not last:
            ctx_mid, s_f, s_b = _mixer_sublayer(ctx, m_ctx[0], m_ctx[1], m_ctx[2], lp,
                                                zero_state, zero_state, grid=False)
        else:
            s_f, s_b = _context_states(ctx, m_ctx[0], m_ctx[1], lp, zero_state)
        x, _, _ = _mixer_sublayer(x, m_lat[0], m_lat[1], m_lat[2], lp, s_f, s_b, grid=True)
        x = x + m_lat[5] * _ec_moe(_modulate(_rmsnorm(x, lp['norm2_g']), m_lat[3], m_lat[4]), lp)
        if not last:
            ctx = ctx_mid + m_ctx[5] * _ec_moe(
                _modulate(_rmsnorm(ctx_mid, lp['norm2_g']), m_ctx[3], m_ctx[4]), lp)
    return _rmsnorm(x, final_g)
```

```python
import functools

import numpy as np
import jax
import jax.numpy as jnp
from jax import lax
from jax.experimental import pallas as pl
from jax.experimental.pallas import tpu as pltpu

F32 = jnp.float32
BF16 = jnp.bfloat16
I32 = jnp.int32
HIGHEST = lax.Precision.HIGHEST

D_MODEL = 2048
GRID_W = 64
RWKV_WIDTH = 1024
RWKV_HEAD = 64
RWKV_HEADS = RWKV_WIDTH // RWKV_HEAD
DECAY_LORA = 64
AAA_LORA = 64
GATE_LORA = 128
GN_EPS = 64e-5
CONV_WIDTH = 1024
CONV_K = 31
LN_EPS = 1e-5
N_EXPERTS = 16
EXPERT_FF = 1024
EC_CAPACITY = 2
N_MOD = 6
RMS_EPS = 1e-6

OFF_WA_F = 3 * RWKV_WIDTH
OFF_WA_B = OFF_WA_F + DECAY_LORA + AAA_LORA
OFF_GLAT = OFF_WA_B + DECAY_LORA + AAA_LORA
OFF_GLU = OFF_GLAT + GATE_LORA
OFF_GATE = OFF_GLU + 2 * CONV_WIDTH
IN_COLS = OFF_GATE + 2 * D_MODEL

LANES = 128
SUBLANES = 8
BF16_ROWS = 16
VMEM_LIMIT_BYTES = 56 * 1024 * 1024

LORA_W = DECAY_LORA + AAA_LORA
PC_GATE = 0
PC_GLU = 2 * D_MODEL
PC_RKV = PC_GLU + 2 * CONV_WIDTH
PC_WA = PC_RKV + 3 * RWKV_WIDTH
PC_GLAT = PC_WA + 2 * LORA_W
assert PC_GLAT + GATE_LORA == IN_COLS
assert PC_GLU % (2 * CONV_WIDTH) == 0 and PC_RKV % (3 * RWKV_WIDTH) == 0 and PC_WA % LORA_W == 0

JQ = RWKV_HEAD // SUBLANES
assert JQ * RWKV_HEADS == LANES


def _channel_perm():
    p = np.arange(RWKV_WIDTH)
    j2, jq, h = p // LANES, (p % LANES) // RWKV_HEADS, p % RWKV_HEADS
    return (h * RWKV_HEAD + j2 * JQ + jq).astype(np.int32)


PERM = _channel_perm()


def _in_col_order():
    return np.concatenate([
        OFF_GATE + np.arange(2 * D_MODEL), OFF_GLU + np.arange(2 * CONV_WIDTH),
        PERM, RWKV_WIDTH + PERM, 2 * RWKV_WIDTH + PERM,
        OFF_WA_F + np.arange(LORA_W), OFF_WA_B + np.arange(LORA_W),
        OFF_GLAT + np.arange(GATE_LORA)]).astype(np.int32)


IN_COL_ORDER = _in_col_order()


def _cparams(sem):
    return pltpu.CompilerParams(dimension_semantics=sem, vmem_limit_bytes=VMEM_LIMIT_BYTES)


def _sigmoid(x):
    return 1.0 / (1.0 + jnp.exp(-x))


def _softplus(x):
    return jnp.maximum(x, 0.0) + jnp.log1p(jnp.exp(-jnp.abs(x)))


def _group_allreduce(x):
    x = x + pltpu.roll(x, RWKV_HEADS, 1)
    x = x + pltpu.roll(x, 2 * RWKV_HEADS, 1)
    return x + pltpu.roll(x, 4 * RWKV_HEADS, 1)


def _headsum(x):
    s = x[:, 0:LANES]
    for i in range(1, SUBLANES):
        s = s + x[:, i * LANES:(i + 1) * LANES]
    s = _group_allreduce(s)
    return jnp.concatenate([s] * SUBLANES, axis=1)


def _mod_kernel(c_ref, w_ref, b_ref, o_ref):
    c = c_ref[...]
    s = c * _sigmoid(c)
    o_ref[...] = jnp.dot(s, w_ref[...], precision=HIGHEST, preferred_element_type=F32) + b_ref[...]


def _modulation(c_rows, w_mod, b_mod):
    depth, d, n = w_mod.shape
    rows = c_rows.shape[0]
    tn = 1024
    return pl.pallas_call(
        _mod_kernel,
        grid=(depth, n // tn),
        in_specs=[pl.BlockSpec((rows, d), lambda l, j: (0, 0)),
                  pl.BlockSpec((None, d, tn), lambda l, j: (l, 0, j)),
                  pl.BlockSpec((None, 1, tn), lambda l, j: (l, 0, j))],
        out_specs=pl.BlockSpec((None, rows, tn), lambda l, j: (l, 0, j)),
        out_shape=jax.ShapeDtypeStruct((depth, rows, n), F32),
        compiler_params=_cparams(("parallel", "parallel")),
        name="modulation",
    )(c_rows, w_mod, b_mod.reshape(depth, 1, n))


def _norm_mod(x, g, shift, scale):
    ms = jnp.mean(x * x, axis=-1, keepdims=True)
    h = x * lax.rsqrt(ms + RMS_EPS) * g
    return h * (1.0 + scale) + shift


def _in_kernel(x_ref, g_ref, sh_ref, sc_ref, w_ref, o_ref):
    h = _norm_mod(x_ref[...], g_ref[...], sh_ref[...], sc_ref[...])
    o_ref[...] = jnp.dot(h.astype(BF16), w_ref[...], preferred_element_type=F32)


def _in_proj(x, g, shift, scale, w_bf16):
    b, l, d = x.shape
    n = w_bf16.shape[1]
    tm = min(l, 512)
    tn = 1920
    return pl.pallas_call(
        _in_kernel,
        grid=(n // tn, b, l // tm),
        in_specs=[pl.BlockSpec((None, tm, d), lambda j, bi, i: (bi, i, 0)),
                  pl.BlockSpec((1, d), lambda j, bi, i: (0, 0)),
                  pl.BlockSpec((None, 1, d), lambda j, bi, i: (bi, 0, 0)),
                  pl.BlockSpec((None, 1, d), lambda j, bi, i: (bi, 0, 0)),
                  pl.BlockSpec((d, tn), lambda j, bi, i: (0, j))],
        out_specs=pl.BlockSpec((None, tm, tn), lambda j, bi, i: (bi, i, j)),
        out_shape=jax.ShapeDtypeStruct((b, l, n), F32),
        compiler_params=_cparams(("parallel", "parallel", "parallel")),
        name="in_proj",
    )(x, g, shift, scale, w_bf16)


def _prep_kernel(direction, rkv_ref, rkv_halo_ref, wa_ref, wa_halo_ref, mu_rkv_ref, mu_wa_ref,
                 w0_ref, a0_ref, kk_ref, ka_ref, rk_ref, wl_ref,
                 r_out, k_out, v_out, w_out, kk_out, kka_out, bonus_out):
    i = pl.program_id(1)
    last = pl.num_programs(1) - 1
    tm = rkv_ref.shape[0]
    rows = lax.broadcasted_iota(I32, (tm, 1), 0)

    def shifted(ref, halo_ref):
        z = ref[...]
        if direction == 0:
            halo = halo_ref[SUBLANES - 1:SUBLANES, :] * jnp.where(i == 0, 0.0, 1.0)
            return z, jnp.where(rows == 0, halo, pltpu.roll(z, 1, 0))
        halo = halo_ref[0:1, :] * jnp.where(i == last, 0.0, 1.0)
        return z, jnp.where(rows == tm - 1, halo, pltpu.roll(z, tm - 1, 0))

    z, zs = shifted(rkv_ref, rkv_halo_ref)
    zm = z + mu_rkv_ref[...] * (zs - z)
    r = zm[:, 0:RWKV_WIDTH]
    k = zm[:, RWKV_WIDTH:2 * RWKV_WIDTH]
    v = zm[:, 2 * RWKV_WIDTH:]
    wa, was = shifted(wa_ref, wa_halo_ref)
    wam = wa + mu_wa_ref[...] * (was - wa)
    lane = lax.broadcasted_iota(I32, wam.shape, 1)
    lat = jnp.where(lane < DECAY_LORA, jnp.tanh(wam), wam)
    lo = jnp.dot(lat.astype(BF16), wl_ref[...], preferred_element_type=F32)
    w_log = -_softplus(-(w0_ref[...] + lo[:, 0:RWKV_WIDTH])) - 0.5
    decay = jnp.exp(-jnp.exp(w_log))
    a = _sigmoid(a0_ref[...] + lo[:, RWKV_WIDTH:])
    kk = k * kk_ref[...]
    kk = kk / jnp.maximum(jnp.sqrt(_headsum(kk * kk)), 1e-12)
    k2 = k * (1.0 + (a - 1.0) * ka_ref[...])
    r_out[...] = r
    k_out[...] = k2
    v_out[...] = v
    w_out[...] = decay
    kk_out[...] = kk
    kka_out[...] = kk * a
    bonus_out[...] = _headsum(r * k2 * rk_ref[...]) * v


def _prep(direction, p, mu_rkv, mu_wa, w0, a0, k_k, k_a, r_k, w_lora):
    b, l, _ = p.shape
    tm = min(l, 256)
    nt = l // tm
    hb = tm // SUBLANES
    nh = l // SUBLANES
    if direction == 0:
        halo_idx = lambda i: jnp.maximum(i * hb - 1, 0)
    else:
        halo_idx = lambda i: jnp.minimum((i + 1) * hb, nh - 1)
    rkv_blk = PC_RKV // (3 * RWKV_WIDTH)
    wa_blk = PC_WA // LORA_W + direction
    vec = lambda n: pl.BlockSpec((1, n), lambda bi, i: (0, 0))
    out = jax.ShapeDtypeStruct((b, l, RWKV_WIDTH), F32)
    return pl.pallas_call(
        functools.partial(_prep_kernel, direction),
        grid=(b, nt),
        in_specs=[pl.BlockSpec((None, tm, 3 * RWKV_WIDTH), lambda bi, i: (bi, i, rkv_blk)),
                  pl.BlockSpec((None, SUBLANES, 3 * RWKV_WIDTH), lambda bi, i: (bi, halo_idx(i), rkv_blk)),
                  pl.BlockSpec((None, tm, LORA_W), lambda bi, i: (bi, i, wa_blk)),
                  pl.BlockSpec((None, SUBLANES, LORA_W), lambda bi, i: (bi, halo_idx(i), wa_blk)),
                  vec(3 * RWKV_WIDTH), vec(LORA_W), vec(RWKV_WIDTH), vec(RWKV_WIDTH), vec(RWKV_WIDTH),
                  vec(RWKV_WIDTH), vec(RWKV_WIDTH),
                  pl.BlockSpec((LORA_W, 2 * RWKV_WIDTH), lambda bi, i: (0, 0))],
        out_specs=[pl.BlockSpec((None, tm, RWKV_WIDTH), lambda bi, i: (bi, i, 0))] * 7,
        out_shape=[out] * 7,
        compiler_params=_cparams(("parallel", "parallel")),
        name="rwkv_prep",
    )(p, p, p, p, mu_rkv, mu_wa, w0, a0, k_k, k_a, r_k, w_lora)


def _scan_kernel(nb, tt, *refs):
    fwd, bwd = refs[0:6], refs[6:12]
    s0_ref, yf_ref, yb_ref, st_ref, s_ref, sa_ref = refs[12:18]
    i = pl.program_id(0)
    lane_grp = lax.broadcasted_iota(I32, (SUBLANES, LANES), 1) // RWKV_HEADS

    @pl.when(i == 0)
    def _():
        s_ref[...] = s0_ref[...]

    def row(ref, b, t, k2):
        return jnp.broadcast_to(ref[b, pl.ds(t, 1), k2, :], (SUBLANES, LANES))

    groups = []
    for d, (streams, y_ref) in enumerate(((fwd, yf_ref), (bwd, yb_ref))):
        for b in range(nb):
            groups.append((d * nb + b, d, b, streams, y_ref))

    for g, d, b, streams, _ in groups:
        kk_ref = streams[4]
        t0 = 0 if d == 0 else tt - 1
        for vq in range(JQ):
            acc = s_ref[g, vq, 0] * row(kk_ref, b, t0, 0)
            for k2 in range(1, SUBLANES):
                acc = acc + s_ref[g, vq, k2] * row(kk_ref, b, t0, k2)
            sa_ref[g, vq] = _group_allreduce(acc)

    def step(t, carry):
        for g, d, b, streams, y_ref in groups:
            r_ref, k_ref, v_ref, w_ref, kk_ref, ka_ref = streams
            if d == 0:
                tc = t
                tn = jnp.minimum(t + 1, tt - 1)
            else:
                tc = tt - 1 - t
                tn = jnp.maximum(tc - 1, 0)
            vt = v_ref[b, tc]
            ytile = jnp.zeros((SUBLANES, LANES), F32)
            for vq in range(JQ):
                vb = _group_allreduce(jnp.where(lane_grp == vq, vt, 0.0))
                sa = sa_ref[g, vq]
                acc_sa = None
                acc_y = None
                for k2 in range(SUBLANES):
                    s = s_ref[g, vq, k2]
                    new = s * row(w_ref, b, tc, k2) - sa * row(ka_ref, b, tc, k2) + vb * row(k_ref, b, tc, k2)
                    s_ref[g, vq, k2] = new
                    psa = new * row(kk_ref, b, tn, k2)
                    py = new * row(r_ref, b, tc, k2)
                    acc_sa = psa if acc_sa is None else acc_sa + psa
                    acc_y = py if acc_y is None else acc_y + py
                sa_ref[g, vq] = _group_allreduce(acc_sa)
                ytile = jnp.where(lane_grp == vq, _group_allreduce(acc_y), ytile)
            y_ref[b, tc] = ytile
        return carry

    lax.fori_loop(0, tt, step, 0)

    @pl.when(i == pl.num_programs(0) - 1)
    def _():
        st_ref[...] = s_ref[...]


def _scan(fwd_streams, bwd_streams, s0):
    b, l, _ = fwd_streams[0].shape
    tt = min(l, 64)
    nt = l // tt
    tile = lambda a: a.reshape(b, l, SUBLANES, LANES)
    f_spec = pl.BlockSpec((b, tt, SUBLANES, LANES), lambda i: (0, i, 0, 0))
    b_spec = pl.BlockSpec((b, tt, SUBLANES, LANES), lambda i: (0, nt - 1 - i, 0, 0))
    s_shape = (2 * b, JQ, SUBLANES, SUBLANES, LANES)
    s_spec = pl.BlockSpec(s_shape, lambda i: (0, 0, 0, 0, 0))
    y_shape = jax.ShapeDtypeStruct((b, l, SUBLANES, LANES), F32)
    yf, yb, st = pl.pallas_call(
        functools.partial(_scan_kernel, b, tt),
        grid=(nt,),
        in_specs=[f_spec] * 6 + [b_spec] * 6 + [s_spec],
        out_specs=[f_spec, b_spec, s_spec],
        out_shape=[y_shape, y_shape, jax.ShapeDtypeStruct(s_shape, F32)],
        scratch_shapes=[pltpu.VMEM(s_shape, F32), pltpu.VMEM((2 * b, JQ, SUBLANES, LANES), F32)],
        compiler_params=_cparams(("arbitrary",)),
        name="rwkv_scan",
    )(*[tile(a) for a in fwd_streams], *[tile(a) for a in bwd_streams], s0)
    return yf.reshape(b, l, RWKV_WIDTH), yb.reshape(b, l, RWKV_WIDTH), st


CONV_PAD = 16


def _mix_kernel(seg, yf_ref, yb_ref, bf_ref, bb_ref, gate_ref, glu_ref, glat_ref,
                lng_ref, lnb_ref, g2_ref, wro_ref, cw_ref, cb_ref, clg_ref, clb_ref, wco_ref,
                o_ref, upad_ref):
    tm = yf_ref.shape[0]
    nseg = tm // seg
    y = yf_ref[...] + yb_ref[...]
    mu = _headsum(y) * (1.0 / RWKV_HEAD)
    yc = y - mu
    var = _headsum(yc * yc) * (1.0 / RWKV_HEAD)
    yn = yc * lax.rsqrt(var + GN_EPS) * lng_ref[...] + lnb_ref[...]
    yn = yn + bf_ref[...] + bb_ref[...]
    g = jnp.dot(_sigmoid(glat_ref[...]).astype(BF16), g2_ref[...], preferred_element_type=F32)
    b_r = jnp.dot((yn * g).astype(BF16), wro_ref[...], preferred_element_type=F32)
    u = glu_ref[:, 0:CONV_WIDTH] * _sigmoid(glu_ref[:, CONV_WIDTH:])
    zpad = jnp.zeros((CONV_PAD, CONV_WIDTH), F32)
    for s in range(nseg):
        upad_ref[s, 0:CONV_PAD, :] = zpad
        upad_ref[s, CONV_PAD:CONV_PAD + seg, :] = u[s * seg:(s + 1) * seg, :]
        upad_ref[s, CONV_PAD + seg:, :] = zpad
    half = CONV_K // 2
    convs = []
    for s in range(nseg):
        acc = None
        for j in range(CONV_K):
            off = CONV_PAD - half + j
            term = upad_ref[s, off:off + seg, :] * cw_ref[j:j + 1, :]
            acc = term if acc is None else acc + term
        convs.append(acc)
    uc = (convs[0] if nseg == 1 else jnp.concatenate(convs, axis=0)) + cb_ref[...]
    m1 = jnp.mean(uc, axis=-1, keepdims=True)
    ucc = uc - m1
    v1 = jnp.mean(ucc * ucc, axis=-1, keepdims=True)
    un = ucc * lax.rsqrt(v1 + LN_EPS) * clg_ref[...] + clb_ref[...]
    b_c = jnp.dot((un * _sigmoid(un)).astype(BF16), wco_ref[...], preferred_element_type=F32)
    g_r = _sigmoid(gate_ref[:, 0:D_MODEL])
    g_c = _sigmoid(gate_ref[:, D_MODEL:])
    o_ref[...] = (g_r * b_r + g_c * b_c).astype(BF16)


def _mix(seg, yf, yb, bonus_f, bonus_b, p, lnx_g, lnx_b, g2, w_ro, conv_w, conv_b, cln_g, cln_b, w_co):
    b, l, _ = yf.shape
    tm = min(l, 256)
    assert tm % seg == 0
    tok = lambda n, blk=0: pl.BlockSpec((None, tm, n), lambda bi, i, blk=blk: (bi, i, blk))
    full = lambda a: pl.BlockSpec(a.shape, lambda bi, i: (0,) * a.ndim)
    params = (lnx_g, lnx_b, g2, w_ro, conv_w, conv_b, cln_g, cln_b, w_co)
    return pl.pallas_call(
        functools.partial(_mix_kernel, seg),
        grid=(b, l // tm),
        in_specs=[tok(RWKV_WIDTH)] * 4 + [tok(2 * D_MODEL, PC_GATE // (2 * D_MODEL)),
                                          tok(2 * CONV_WIDTH, PC_GLU // (2 * CONV_WIDTH)),
                                          tok(GATE_LORA, PC_GLAT // GATE_LORA)]
                 + [full(a) for a in params],
        out_specs=tok(D_MODEL),
        out_shape=jax.ShapeDtypeStruct((b, l, D_MODEL), BF16),
        scratch_shapes=[pltpu.VMEM((tm // seg, seg + 2 * CONV_PAD, CONV_WIDTH), F32)],
        compiler_params=_cparams(("parallel", "parallel")),
        name="mixer_branches",
    )(yf, yb, bonus_f, bonus_b, p, p, p, *params)


def _out_kernel(m_ref, w_ref, x_ref, gate_ref, o_ref):
    o_ref[...] = x_ref[...] + gate_ref[...] * jnp.dot(m_ref[...], w_ref[...], preferred_element_type=F32)


def _out_proj(m, w_o, x, gate):
    b, l, d = x.shape
    tm = min(l, 512)
    tok = pl.BlockSpec((None, tm, d), lambda bi, i: (bi, i, 0))
    return pl.pallas_call(
        _out_kernel,
        grid=(b, l // tm),
        in_specs=[tok, pl.BlockSpec((d, d), lambda bi, i: (0, 0)), tok,
                  pl.BlockSpec((None, 1, d), lambda bi, i: (bi, 0, 0))],
        out_specs=tok,
        out_shape=jax.ShapeDtypeStruct((b, l, d), F32),
        compiler_params=_cparams(("parallel", "parallel")),
        name="mixer_out",
    )(m, w_o, x, gate)


def _route_kernel(x_ref, g_ref, sh_ref, sc_ref, wr_ref, hn_ref, aff_ref):
    h = _norm_mod(x_ref[...], g_ref[...], sh_ref[...], sc_ref[...])
    hn_ref[...] = h.astype(BF16)
    logits = lax.dot_general(wr_ref[...], h, (((1,), (1,)), ((), ())), precision=HIGHEST,
                             preferred_element_type=F32)
    m = jnp.max(logits, axis=0, keepdims=True)
    e = jnp.exp(logits - m)
    aff_ref[...] = e / jnp.sum(e, axis=0, keepdims=True)


def _route(x, g, shift, scale, w_router_t):
    b, l, d = x.shape
    tm = min(l, 512)
    row = pl.BlockSpec((None, 1, d), lambda bi, i: (bi, 0, 0))
    return pl.pallas_call(
        _route_kernel,
        grid=(b, l // tm),
        in_specs=[pl.BlockSpec((None, tm, d), lambda bi, i: (bi, i, 0)),
                  pl.BlockSpec((1, d), lambda bi, i: (0, 0)), row, row,
                  pl.BlockSpec((N_EXPERTS, d), lambda bi, i: (0, 0))],
        out_specs=[pl.BlockSpec((None, tm, d), lambda bi, i: (bi, i, 0)),
                   pl.BlockSpec((None, N_EXPERTS, tm), lambda bi, i: (bi, 0, i))],
        out_shape=[jax.ShapeDtypeStruct((b, l, d), BF16), jax.ShapeDtypeStruct((b, N_EXPERTS, l), F32)],
        compiler_params=_cparams(("parallel", "parallel")),
        name="moe_route",
    )(x, g, shift, scale, w_router_t)


SEL_BLK = LANES


def _select_kernel(cap, aff_ref, slot_ref, off_ref):
    ne, l = aff_ref.shape
    nblk = l // SEL_BLK

    def bits():
        return pltpu.bitcast(aff_ref[...], I32)

    def count(mask_i32):
        return jnp.sum(mask_i32, axis=1, keepdims=True)

    def value_step(it, prefix):
        cand = prefix | lax.shift_left(jnp.int32(1), 30 - it)
        cnt = count(jnp.where(bits() >= cand, 1, 0))
        return jnp.where(cnt >= cap, cand, prefix)

    tau = lax.fori_loop(0, 31, value_step, jnp.zeros((ne, 1), I32))
    need = cap - count(jnp.where(bits() > tau, 1, 0))
    tok = lax.broadcasted_iota(I32, (ne, l), 1)
    nbits = int(l).bit_length()

    def index_step(it, tp):
        cand = tp | lax.shift_left(jnp.int32(1), nbits - 1 - it)
        f = count(jnp.where((bits() == tau) & (tok < cand), 1, 0))
        return jnp.where(f < need, cand, tp)

    tlast = lax.fori_loop(0, nbits, index_step, jnp.zeros((ne, 1), I32))
    bt = bits()
    sel = (bt > tau) | ((bt == tau) & (tok <= tlast))
    sel_b = jnp.where(sel, 1.0, 0.0).astype(BF16)
    blk_of = lax.broadcasted_iota(I32, (l, nblk), 0) // SEL_BLK
    blk_id = lax.broadcasted_iota(I32, (l, nblk), 1)
    tot = jnp.dot(sel_b, jnp.where(blk_of == blk_id, 1.0, 0.0).astype(BF16), preferred_element_type=F32)
    ra = lax.broadcasted_iota(I32, (nblk, nblk), 0)
    ca = lax.broadcasted_iota(I32, (nblk, nblk), 1)
    offs = jnp.dot(tot.astype(BF16), jnp.where(ra < ca, 1.0, 0.0).astype(BF16), preferred_element_type=F32)
    off_ref[...] = offs.astype(I32)
    ri = lax.broadcasted_iota(I32, (SEL_BLK, SEL_BLK), 0)
    ci = lax.broadcasted_iota(I32, (SEL_BLK, SEL_BLK), 1)
    strict_upper = jnp.where(ri < ci, 1.0, 0.0).astype(BF16)
    for c in range(nblk):
        sl = slice(c * SEL_BLK, (c + 1) * SEL_BLK)
        excl = jnp.dot(sel_b[:, sl], strict_upper, preferred_element_type=F32)
        pos = (excl + offs[:, c:c + 1]).astype(I32)
        slot_ref[:, sl] = jnp.where(sel[:, sl], pos, -1)


def _select(aff, cap):
    b, ne, l = aff.shape
    nblk = l // SEL_BLK
    return pl.pallas_call(
        functools.partial(_select_kernel, cap),
        grid=(b,),
        in_specs=[pl.BlockSpec((None, ne, l), lambda bi: (bi, 0, 0))],
        out_specs=[pl.BlockSpec((None, ne, l), lambda bi: (bi, 0, 0)),
                   pl.BlockSpec((None, ne, nblk), lambda bi: (bi, 0, 0))],
        out_shape=[jax.ShapeDtypeStruct((b, ne, l), I32), jax.ShapeDtypeStruct((b, ne, nblk), I32)],
        compiler_params=_cparams(("parallel",)),
        name="moe_select",
    )(aff)


GATHER_TS = 256
GATHER_WIN = GATHER_TS + BF16_ROWS
COMBINE_TS = SEL_BLK
COMBINE_WIN = COMBINE_TS + BF16_ROWS


def _expert_kernel(cap, nblk, tb, off_ref, slot_ref, hn_ref, wg_ref, wu_ref, wd_ref, y_ref, xs_ref):
    e, b, n = pl.program_id(0), pl.program_id(1), pl.program_id(2)

    @pl.when(n == 0)
    def _():
        xs_ref[...] = jnp.zeros(xs_ref.shape, xs_ref.dtype)

    base = (b * N_EXPERTS + e) * nblk + n * (tb // SEL_BLK)
    srow = lax.broadcasted_iota(I32, (GATHER_WIN, GATHER_TS), 0)
    for j in range(tb // GATHER_TS):
        p0 = off_ref[base + j * (GATHER_TS // SEL_BLK)]
        start = pl.multiple_of((p0 // BF16_ROWS) * BF16_ROWS, BF16_ROWS)
        rel = slot_ref[:, j * GATHER_TS:(j + 1) * GATHER_TS] - start
        onehot = jnp.where(srow == rel, 1.0, 0.0).astype(BF16)
        rows = jnp.dot(onehot, hn_ref[j * GATHER_TS:(j + 1) * GATHER_TS, :], preferred_element_type=F32)
        xs_ref[pl.ds(start, GATHER_WIN), :] = xs_ref[pl.ds(start, GATHER_WIN), :] + rows.astype(BF16)

    @pl.when(n == pl.num_programs(2) - 1)
    def _():
        xs = xs_ref[0:cap, :]
        gh = jnp.dot(xs, wg_ref[...], preferred_element_type=F32)
        uh = jnp.dot(xs, wu_ref[...], preferred_element_type=F32)
        act = (gh * _sigmoid(gh)) * uh
        y_ref[0:cap, :] = jnp.dot(act.astype(BF16), wd_ref[...], preferred_element_type=F32).astype(BF16)
        y_ref[cap:, :] = jnp.zeros((y_ref.shape[0] - cap, y_ref.shape[1]), BF16)


def _experts(off_flat, slot4, hn, wg, wu, wd, cap):
    b, l, d = hn.shape
    nblk = l // SEL_BLK
    tb = min(l, 1024)
    yr = cap + COMBINE_WIN
    grid_spec = pltpu.PrefetchScalarGridSpec(
        num_scalar_prefetch=1,
        grid=(N_EXPERTS, b, l // tb),
        in_specs=[pl.BlockSpec((None, None, 1, tb), lambda e, bi, n, off: (bi, e, 0, n)),
                  pl.BlockSpec((None, tb, d), lambda e, bi, n, off: (bi, n, 0)),
                  pl.BlockSpec((None, d, EXPERT_FF), lambda e, bi, n, off: (e, 0, 0)),
                  pl.BlockSpec((None, d, EXPERT_FF), lambda e, bi, n, off: (e, 0, 0)),
                  pl.BlockSpec((None, EXPERT_FF, d), lambda e, bi, n, off: (e, 0, 0))],
        out_specs=pl.BlockSpec((None, None, yr, d), lambda e, bi, n, off: (bi, e, 0, 0)),
        scratch_shapes=[pltpu.VMEM((cap + GATHER_WIN, d), BF16)])
    return pl.pallas_call(
        functools.partial(_expert_kernel, cap, nblk, tb),
        grid_spec=grid_spec,
        out_shape=jax.ShapeDtypeStruct((b, N_EXPERTS, yr, d), BF16),
        compiler_params=_cparams(("parallel", "parallel", "arbitrary")),
        name="moe_experts",
    )(off_flat, slot4, hn, wg, wu, wd)


def _combine_kernel(nblk, tb, off_ref, slot_ref, aff_ref, y_ref, x_ref, gate_ref, o_ref, acc_ref):
    b, n, e = pl.program_id(0), pl.program_id(1), pl.program_id(2)

    @pl.when(e == 0)
    def _():
        acc_ref[...] = jnp.zeros(acc_ref.shape, F32)

    base = (b * N_EXPERTS + e) * nblk + n * (tb // SEL_BLK)
    srow = lax.broadcasted_iota(I32, (COMBINE_WIN, COMBINE_TS), 0)
    for j in range(tb // COMBINE_TS):
        sl = slice(j * COMBINE_TS, (j + 1) * COMBINE_TS)
        p0 = off_ref[base + j]
        start = pl.multiple_of((p0 // BF16_ROWS) * BF16_ROWS, BF16_ROWS)
        rel = slot_ref[:, sl] - start
        wt = jnp.where(srow == rel, aff_ref[:, sl], 0.0).astype(BF16)
        contrib = lax.dot_general(wt, y_ref[pl.ds(start, COMBINE_WIN), :], (((0,), (0,)), ((), ())),
                                  preferred_element_type=F32)
        acc_ref[sl, :] = acc_ref[sl, :] + contrib

    @pl.when(e == pl.num_programs(2) - 1)
    def _():
        o_ref[...] = x_ref[...] + gate_ref[...] * acc_ref[...]


def _combine(off_flat, slot4, aff4, y, x, gate):
    b, l, d = x.shape
    nblk = l // SEL_BLK
    tb = min(l, 1024)
    yr = y.shape[2]
    tok1 = pl.BlockSpec((None, None, 1, tb), lambda bi, n, e, off: (bi, e, 0, n))
    tokd = pl.BlockSpec((None, tb, d), lambda bi, n, e, off: (bi, n, 0))
    grid_spec = pltpu.PrefetchScalarGridSpec(
        num_scalar_prefetch=1,
        grid=(b, l // tb, N_EXPERTS),
        in_specs=[tok1, tok1,
                  pl.BlockSpec((None, None, yr, d), lambda bi, n, e, off: (bi, e, 0, 0)),
                  tokd, pl.BlockSpec((None, 1, d), lambda bi, n, e, off: (bi, 0, 0))],
        out_specs=tokd,
        scratch_shapes=[pltpu.VMEM((tb, d), F32)])
    return pl.pallas_call(
        functools.partial(_combine_kernel, nblk, tb),
        grid_spec=grid_spec,
        out_shape=jax.ShapeDtypeStruct((b, l, d), F32),
        compiler_params=_cparams(("parallel", "parallel", "arbitrary")),
        name="moe_combine",
    )(off_flat, slot4, aff4, y, x, gate)


def _final_kernel(x_ref, g_ref, o_ref):
    x = x_ref[...]
    ms = jnp.mean(x * x, axis=-1, keepdims=True)
    o_ref[...] = x * lax.rsqrt(ms + RMS_EPS) * g_ref[...]


def _final_norm(x, g):
    b, l, d = x.shape
    tm = min(l, 512)
    tok = pl.BlockSpec((None, tm, d), lambda bi, i: (bi, i, 0))
    return pl.pallas_call(
        _final_kernel,
        grid=(b, l // tm),
        in_specs=[tok, pl.BlockSpec((1, d), lambda bi, i: (0, 0))],
        out_specs=tok,
        out_shape=jax.ShapeDtypeStruct((b, l, d), F32),
        compiler_params=_cparams(("parallel", "parallel")),
        name="final_norm",
    )(x, g)


def _rwkv_streams(p, lw):
    outs = []
    for d in range(2):
        outs.append(_prep(d, p, lw['mu_rkv'][d], lw['mu_wa'][d], lw['w0'][d], lw['a0'][d], lw['k_k'][d],
                          lw['k_a'][d], lw['r_k'], lw['w_lora'][d]))
    return outs


def _mixer(x, mods, lw, s0, seg, tail=True):
    p = _in_proj(x, lw['norm1_g'], mods[0], mods[1], lw['w_in'])
    sf, sb = _rwkv_streams(p, lw)
    yf, yb, st = _scan(sf[:6], sb[:6], s0)
    if not tail:
        return None, st
    m = _mix(seg, yf, yb, sf[6], sb[6], p, lw['lnx_g'], lw['lnx_b'], lw['g2'], lw['w_rwkv_out'],
             lw['conv_w'], lw['conv_b'], lw['conv_ln_g'], lw['conv_ln_b'], lw['w_conv_out'])
    return _out_proj(m, lw['w_o'], x, mods[2]), st


def _moe(x, mods, lw):
    b, l, d = x.shape
    cap = EC_CAPACITY * l // N_EXPERTS
    hn, aff = _route(x, lw['norm2_g'], mods[3], mods[4], lw['w_router_t'])
    slot, off = _select(aff, cap)
    slot4 = slot.reshape(b, N_EXPERTS, 1, l)
    aff4 = aff.reshape(b, N_EXPERTS, 1, l)
    off_flat = off.reshape(-1)
    y = _experts(off_flat, slot4, hn, lw['w_exp_gate'], lw['w_exp_up'], lw['w_exp_down'], cap)
    return _combine(off_flat, slot4, aff4, y, x, mods[5])


def _layer_weights(l, w):
    perm = PERM
    row = lambda a: a.reshape(1, -1)
    mu = w['tok_mu'][l]
    mu_rkv = [row(jnp.concatenate([mu[d, 0:RWKV_WIDTH][perm], mu[d, RWKV_WIDTH:2 * RWKV_WIDTH][perm],
                                   mu[d, 2 * RWKV_WIDTH:3 * RWKV_WIDTH][perm]])) for d in range(2)]
    mu_wa = [row(mu[d, 3 * RWKV_WIDTH:]) for d in range(2)]
    zeros = jnp.zeros((DECAY_LORA, RWKV_WIDTH), F32)
    w_lora = [jnp.concatenate([jnp.concatenate([w['w2'][l, d][:, perm], zeros], axis=1),
                               jnp.concatenate([zeros, w['a2'][l, d][:, perm]], axis=1)], axis=0).astype(BF16)
              for d in range(2)]
    pv = lambda a: [row(a[l, d][perm]) for d in range(2)]
    return dict(
        norm1_g=row(w['norm1_g'][l]),
        w_in=w['w_in'][l][:, IN_COL_ORDER].astype(BF16),
        mu_rkv=mu_rkv, mu_wa=mu_wa, w_lora=w_lora,
        w0=pv(w['w0']), a0=pv(w['a0']), k_k=pv(w['k_k']), k_a=pv(w['k_a']),
        r_k=row(w['r_k'][l][perm]), lnx_g=row(w['lnx_g'][l][perm]), lnx_b=row(w['lnx_b'][l][perm]),
        g2=w['g2'][l][:, perm].astype(BF16),
        w_rwkv_out=w['w_rwkv_out'][l][perm, :].astype(BF16),
        conv_w=w['conv_w'][l], conv_b=row(w['conv_b'][l]), conv_ln_g=row(w['conv_ln_g'][l]),
        conv_ln_b=row(w['conv_ln_b'][l]), w_conv_out=w['w_conv_out'][l].astype(BF16),
        w_o=w['w_o'][l].astype(BF16), norm2_g=row(w['norm2_g'][l]),
        w_router_t=w['w_router'][l].T,
        w_exp_gate=w['w_exp_gate'][l].astype(BF16), w_exp_up=w['w_exp_up'][l].astype(BF16),
        w_exp_down=w['w_exp_down'][l].astype(BF16))


def kernel(x, c, ctx, c_ctx, w_mod, b_mod, norm1_g, w_in, tok_mu, w0, w2, a0, a2, k_k, k_a, r_k, g2, lnx_g,
           lnx_b, w_rwkv_out, conv_w, conv_b, conv_ln_g, conv_ln_b, w_conv_out, w_o, norm2_g, w_router,
           w_exp_gate, w_exp_up, w_exp_down, final_g):
    w = dict(norm1_g=norm1_g, w_in=w_in, tok_mu=tok_mu, w0=w0, w2=w2, a0=a0, a2=a2, k_k=k_k, k_a=k_a, r_k=r_k,
             g2=g2, lnx_g=lnx_g, lnx_b=lnx_b, w_rwkv_out=w_rwkv_out, conv_w=conv_w, conv_b=conv_b,
             conv_ln_g=conv_ln_g, conv_ln_b=conv_ln_b, w_conv_out=w_conv_out, w_o=w_o, norm2_g=norm2_g,
             w_router=w_router, w_exp_gate=w_exp_gate, w_exp_up=w_exp_up, w_exp_down=w_exp_down)
    batch = x.shape[0]
    depth = w_mod.shape[0]
    ctx_len = ctx.shape[1]
    pad_rows = (-(batch + 1)) % SUBLANES
    c_rows = jnp.concatenate([c, c_ctx[None, :], jnp.zeros((pad_rows, D_MODEL), F32)], axis=0)
    mod = _modulation(c_rows, w_mod, b_mod)
    zero_state = jnp.zeros((2 * batch, JQ, SUBLANES, SUBLANES, LANES), F32)
    for l in range(depth):
        lw = _layer_weights(l, w)
        m_lat = [mod[l, 0:batch, i * D_MODEL:(i + 1) * D_MODEL][:, None, :] for i in range(N_MOD)]
        m_ctx = [jnp.broadcast_to(mod[l, batch:batch + 1, i * D_MODEL:(i + 1) * D_MODEL][:, None, :],
                                  (batch, 1, D_MODEL)) for i in range(N_MOD)]
        last = l == depth - 1
        ctx_mid, state = _mixer(ctx, m_ctx, lw, zero_state, ctx_len, tail=not last)
        x, _ = _mixer(x, m_lat, lw, state, GRID_W)
        x = _moe(x, m_lat, lw)
        if not last:
            ctx = _moe(ctx_mid, m_ctx, lw)
    return _final_norm(x, final_g.reshape(1, -1))
```

```python
import functools

import numpy as np
import jax
import jax.numpy as jnp
from jax import lax
from jax.experimental import pallas as pl
from jax.experimental.pallas import tpu as pltpu

F32 = jnp.float32
BF16 = jnp.bfloat16
I32 = jnp.int32
HIGHEST = lax.Precision.HIGHEST

D_MODEL = 2048
GRID_W = 64
RWKV_WIDTH = 1024
RWKV_HEAD = 64
RWKV_HEADS = RWKV_WIDTH // RWKV_HEAD
DECAY_LORA = 64
AAA_LORA = 64
GATE_LORA = 128
GN_EPS = 64e-5
CONV_WIDTH = 1024
CONV_K = 31
LN_EPS = 1e-5
N_EXPERTS = 16
EXPERT_FF = 1024
EC_CAPACITY = 2
N_MOD = 6
RMS_EPS = 1e-6

OFF_WA_F = 3 * RWKV_WIDTH
OFF_WA_B = OFF_WA_F + DECAY_LORA + AAA_LORA
OFF_GLAT = OFF_WA_B + DECAY_LORA + AAA_LORA
OFF_GLU = OFF_GLAT + GATE_LORA
OFF_GATE = OFF_GLU + 2 * CONV_WIDTH
IN_COLS = OFF_GATE + 2 * D_MODEL

LANES = 128
SUBLANES = 8
BF16_ROWS = 16
VMEM_LIMIT_BYTES = 56 * 1024 * 1024

LORA_W = DECAY_LORA + AAA_LORA
PC_GATE = 0
PC_GLU = 2 * D_MODEL
PC_RKV = PC_GLU + 2 * CONV_WIDTH
PC_WA = PC_RKV + 3 * RWKV_WIDTH
PC_GLAT = PC_WA + 2 * LORA_W
assert PC_GLAT + GATE_LORA == IN_COLS
assert PC_GLU % (2 * CONV_WIDTH) == 0 and PC_RKV % (3 * RWKV_WIDTH) == 0 and PC_WA % LORA_W == 0

JQ = RWKV_HEAD // SUBLANES
assert JQ * RWKV_HEADS == LANES


def _channel_perm():
    p = np.arange(RWKV_WIDTH)
    j2, jq, h = p // LANES, (p % LANES) // RWKV_HEADS, p % RWKV_HEADS
    return (h * RWKV_HEAD + j2 * JQ + jq).astype(np.int32)


PERM = _channel_perm()


def _in_col_order():
    return np.concatenate([
        OFF_GATE + np.arange(2 * D_MODEL), OFF_GLU + np.arange(2 * CONV_WIDTH),
        PERM, RWKV_WIDTH + PERM, 2 * RWKV_WIDTH + PERM,
        OFF_WA_F + np.arange(LORA_W), OFF_WA_B + np.arange(LORA_W),
        OFF_GLAT + np.arange(GATE_LORA)]).astype(np.int32)


IN_COL_ORDER = _in_col_order()


def _cparams(sem, flags=None):
    return pltpu.CompilerParams(dimension_semantics=sem, vmem_limit_bytes=VMEM_LIMIT_BYTES, flags=flags)


def _sigmoid(x):
    return 1.0 / (1.0 + jnp.exp(-x))


def _softplus(x):
    return jnp.maximum(x, 0.0) + jnp.log1p(jnp.exp(-jnp.abs(x)))


def _group_allreduce(x):
    x = x + pltpu.roll(x, RWKV_HEADS, 1)
    x = x + pltpu.roll(x, 2 * RWKV_HEADS, 1)
    return x + pltpu.roll(x, 4 * RWKV_HEADS, 1)


def _headsum(x):
    s = x[:, 0:LANES]
    for i in range(1, SUBLANES):
        s = s + x[:, i * LANES:(i + 1) * LANES]
    s = _group_allreduce(s)
    return jnp.concatenate([s] * SUBLANES, axis=1)


def _mod_kernel(c_ref, w_ref, b_ref, o_ref):
    c = c_ref[...]
    s = c * _sigmoid(c)
    o_ref[...] = jnp.dot(s, w_ref[...], precision=HIGHEST, preferred_element_type=F32) + b_ref[...]


def _modulation(c_rows, w_mod, b_mod):
    depth, d, n = w_mod.shape
    rows = c_rows.shape[0]
    tn = 1024
    return pl.pallas_call(
        _mod_kernel,
        grid=(depth, n // tn),
        in_specs=[pl.BlockSpec((rows, d), lambda l, j: (0, 0)),
                  pl.BlockSpec((None, d, tn), lambda l, j: (l, 0, j)),
                  pl.BlockSpec((None, 1, tn), lambda l, j: (l, 0, j))],
        out_specs=pl.BlockSpec((None, rows, tn), lambda l, j: (l, 0, j)),
        out_shape=jax.ShapeDtypeStruct((depth, rows, n), F32),
        compiler_params=_cparams(("parallel", "parallel")),
        name="modulation",
    )(c_rows, w_mod, b_mod.reshape(depth, 1, n))


def _norm_mod(x, g, shift, scale):
    ms = jnp.mean(x * x, axis=-1, keepdims=True)
    h = x * lax.rsqrt(ms + RMS_EPS) * g
    return h * (1.0 + scale) + shift


def _in_kernel(x_ref, g_ref, sh_ref, sc_ref, w_ref, o_ref):
    h = _norm_mod(x_ref[...], g_ref[...], sh_ref[...], sc_ref[...])
    o_ref[...] = jnp.dot(h.astype(BF16), w_ref[...], preferred_element_type=F32)


def _in_proj(x, g, shift, scale, w_bf16):
    b, l, d = x.shape
    n = w_bf16.shape[1]
    tm = min(l, 512)
    tn = 1920
    return pl.pallas_call(
        _in_kernel,
        grid=(n // tn, b, l // tm),
        in_specs=[pl.BlockSpec((None, tm, d), lambda j, bi, i: (bi, i, 0)),
                  pl.BlockSpec((1, d), lambda j, bi, i: (0, 0)),
                  pl.BlockSpec((None, 1, d), lambda j, bi, i: (bi, 0, 0)),
                  pl.BlockSpec((None, 1, d), lambda j, bi, i: (bi, 0, 0)),
                  pl.BlockSpec((d, tn), lambda j, bi, i: (0, j))],
        out_specs=pl.BlockSpec((None, tm, tn), lambda j, bi, i: (bi, i, j)),
        out_shape=jax.ShapeDtypeStruct((b, l, n), F32),
        compiler_params=_cparams(("parallel", "parallel", "parallel")),
        name="in_proj",
    )(x, g, shift, scale, w_bf16)


def _prep_kernel(direction, rkv_ref, rkv_halo_ref, wa_ref, wa_halo_ref, mu_rkv_ref, mu_wa_ref,
                 w0_ref, a0_ref, kk_ref, ka_ref, rk_ref, wl_ref,
                 r_out, k_out, v_out, w_out, kk_out, kka_out, bonus_out):
    i = pl.program_id(1)
    last = pl.num_programs(1) - 1
    tm = rkv_ref.shape[0]
    rows = lax.broadcasted_iota(I32, (tm, 1), 0)

    def shifted(ref, halo_ref):
        z = ref[...]
        if direction == 0:
            halo = halo_ref[SUBLANES - 1:SUBLANES, :] * jnp.where(i == 0, 0.0, 1.0)
            return z, jnp.where(rows == 0, halo, pltpu.roll(z, 1, 0))
        halo = halo_ref[0:1, :] * jnp.where(i == last, 0.0, 1.0)
        return z, jnp.where(rows == tm - 1, halo, pltpu.roll(z, tm - 1, 0))

    z, zs = shifted(rkv_ref, rkv_halo_ref)
    zm = z + mu_rkv_ref[...] * (zs - z)
    r = zm[:, 0:RWKV_WIDTH]
    k = zm[:, RWKV_WIDTH:2 * RWKV_WIDTH]
    v = zm[:, 2 * RWKV_WIDTH:]
    wa, was = shifted(wa_ref, wa_halo_ref)
    wam = wa + mu_wa_ref[...] * (was - wa)
    lane = lax.broadcasted_iota(I32, wam.shape, 1)
    lat = jnp.where(lane < DECAY_LORA, jnp.tanh(wam), wam)
    lo = jnp.dot(lat.astype(BF16), wl_ref[...], preferred_element_type=F32)
    w_log = -_softplus(-(w0_ref[...] + lo[:, 0:RWKV_WIDTH])) - 0.5
    decay = jnp.exp(-jnp.exp(w_log))
    a = _sigmoid(a0_ref[...] + lo[:, RWKV_WIDTH:])
    kk = k * kk_ref[...]
    kk = kk / jnp.maximum(jnp.sqrt(_headsum(kk * kk)), 1e-12)
    k2 = k * (1.0 + (a - 1.0) * ka_ref[...])
    r_out[...] = r
    k_out[...] = k2
    v_out[...] = v
    w_out[...] = decay
    kk_out[...] = kk
    kka_out[...] = kk * a
    bonus_out[...] = _headsum(r * k2 * rk_ref[...]) * v


def _prep(direction, p, mu_rkv, mu_wa, w0, a0, k_k, k_a, r_k, w_lora):
    b, l, _ = p.shape
    tm = min(l, 256)
    nt = l // tm
    hb = tm // SUBLANES
    nh = l // SUBLANES
    if direction == 0:
        halo_idx = lambda i: jnp.maximum(i * hb - 1, 0)
    else:
        halo_idx = lambda i: jnp.minimum((i + 1) * hb, nh - 1)
    rkv_blk = PC_RKV // (3 * RWKV_WIDTH)
    wa_blk = PC_WA // LORA_W + direction
    vec = lambda n: pl.BlockSpec((1, n), lambda bi, i: (0, 0))
    out = jax.ShapeDtypeStruct((b, l, RWKV_WIDTH), F32)
    return pl.pallas_call(
        functools.partial(_prep_kernel, direction),
        grid=(b, nt),
        in_specs=[pl.BlockSpec((None, tm, 3 * RWKV_WIDTH), lambda bi, i: (bi, i, rkv_blk)),
                  pl.BlockSpec((None, SUBLANES, 3 * RWKV_WIDTH), lambda bi, i: (bi, halo_idx(i), rkv_blk)),
                  pl.BlockSpec((None, tm, LORA_W), lambda bi, i: (bi, i, wa_blk)),
                  pl.BlockSpec((None, SUBLANES, LORA_W), lambda bi, i: (bi, halo_idx(i), wa_blk)),
                  vec(3 * RWKV_WIDTH), vec(LORA_W), vec(RWKV_WIDTH), vec(RWKV_WIDTH), vec(RWKV_WIDTH),
                  vec(RWKV_WIDTH), vec(RWKV_WIDTH),
                  pl.BlockSpec((LORA_W, 2 * RWKV_WIDTH), lambda bi, i: (0, 0))],
        out_specs=[pl.BlockSpec((None, tm, RWKV_WIDTH), lambda bi, i: (bi, i, 0))] * 7,
        out_shape=[out] * 7,
        compiler_params=_cparams(("parallel", "parallel")),
        name="rwkv_prep",
    )(p, p, p, p, mu_rkv, mu_wa, w0, a0, k_k, k_a, r_k, w_lora)


def _reduce_scatter_level(cur, lane):
    half = len(cur) // 2
    width = RWKV_HEADS * half
    low = (lane % (2 * width)) < width
    nxt = []
    for j in range(half):
        keep = jnp.where(low, cur[j], cur[j + half])
        other = jnp.where(low, cur[j + half], cur[j])
        if 2 * width == LANES:
            moved = pltpu.roll(other, width, 1)
        else:
            moved = jnp.where(low, pltpu.roll(other, LANES - width, 1), pltpu.roll(other, width, 1))
        nxt.append(keep + moved)
    return nxt


def _scan_kernel(nb, tt, *refs):
    fwd, bwd = refs[0:6], refs[6:12]
    e3_ref, s0_ref, yf_ref, yb_ref, st_ref, s_ref, sa_ref, qr_ref, yr_ref, vb_ref, wkk_ref = refs[12:]
    i = pl.program_id(0)
    lane = lax.broadcasted_iota(I32, (SUBLANES, LANES), 1)
    tile = (SUBLANES, LANES)

    @pl.when(i == 0)
    def _():
        s_ref[...] = s0_ref[...]

    def row(ref, b, t, k2):
        return jnp.broadcast_to(ref[b, pl.ds(t, 1), k2, :], tile)

    def wkk_row(g, k2):
        return jnp.broadcast_to(wkk_ref[g, k2:k2 + 1, :], tile)

    groups = []
    for d, (streams, y_ref) in enumerate(((fwd, yf_ref), (bwd, yb_ref))):
        for b in range(nb):
            groups.append((d * nb + b, d, b, streams, y_ref))

    def flush_y(g, b, y_ref, t):
        cur = [yr_ref[g, vq] for vq in range(JQ)]
        while len(cur) > 1:
            cur = _reduce_scatter_level(cur, lane)
        y_ref[b, t] = cur[0]

    def head_dot(x, y):
        return _group_allreduce(jnp.broadcast_to(jnp.sum(x * y, axis=0, keepdims=True), tile))

    for g, d, b, streams, _ in groups:
        _, _, v_ref, w_ref, kk_ref, _ = streams
        v = v_ref[b].reshape(tt * SUBLANES, LANES)
        hi = v.astype(BF16)
        r1 = v - hi.astype(F32)
        mid = r1.astype(BF16)
        lo = (r1 - mid.astype(F32)).astype(BF16)
        vb = jnp.dot(jnp.concatenate([hi, mid, lo], axis=1), e3_ref[...], preferred_element_type=F32)
        vb_ref[g] = vb.reshape(tt, SUBLANES, JQ * LANES)
        t0, t1 = (0, 1) if d == 0 else (tt - 1, tt - 2)
        wkk_ref[g] = w_ref[b, t0] * kk_ref[b, t1]
        for vq in range(JQ):
            acc_sa = None
            acc_q = None
            for k2 in range(SUBLANES):
                s = s_ref[g, vq, k2]
                psa = s * row(kk_ref, b, t0, k2)
                pq = s * wkk_row(g, k2)
                acc_sa = psa if acc_sa is None else acc_sa + psa
                acc_q = pq if acc_q is None else acc_q + pq
            sa_ref[g, vq] = _group_allreduce(acc_sa)
            qr_ref[g, vq] = acc_q
            yr_ref[g, vq] = jnp.zeros(tile, F32)

    def step(t, carry):
        for g, d, b, streams, y_ref in groups:
            r_ref, k_ref, v_ref, w_ref, kk_ref, ka_ref = streams
            if d == 0:
                tc = t
                tp = jnp.maximum(t - 1, 0)
                n1 = jnp.minimum(t + 1, tt - 1)
                n2 = jnp.minimum(t + 2, tt - 1)
            else:
                tc = tt - 1 - t
                tp = jnp.minimum(tc + 1, tt - 1)
                n1 = jnp.maximum(tc - 1, 0)
                n2 = jnp.maximum(tc - 2, 0)
            flush_y(g, b, y_ref, tp)
            kk1 = kk_ref[b, n1]
            alpha = head_dot(ka_ref[b, tc], kk1)
            beta = head_dot(k_ref[b, tc], kk1)
            q_prev = [_group_allreduce(qr_ref[g, vq]) for vq in range(JQ)]
            wkk_ref[g] = w_ref[b, n1] * kk_ref[b, n2]
            sa = [sa_ref[g, vq] for vq in range(JQ)]
            vbs = [vb_ref[g, tc, :, vq * LANES:(vq + 1) * LANES] for vq in range(JQ)]
            acc_y = [None] * JQ
            acc_q = [None] * JQ
            for k2 in range(SUBLANES):
                w_r = row(w_ref, b, tc, k2)
                ka_r = row(ka_ref, b, tc, k2)
                k_r = row(k_ref, b, tc, k2)
                r_r = row(r_ref, b, tc, k2)
                wkk_r = wkk_row(g, k2)
                for vq in range(JQ):
                    new = s_ref[g, vq, k2] * w_r - sa[vq] * ka_r + vbs[vq] * k_r
                    s_ref[g, vq, k2] = new
                    py = new * r_r
                    pq = new * wkk_r
                    acc_y[vq] = py if acc_y[vq] is None else acc_y[vq] + py
                    acc_q[vq] = pq if acc_q[vq] is None else acc_q[vq] + pq
            for vq in range(JQ):
                yr_ref[g, vq] = acc_y[vq]
                qr_ref[g, vq] = acc_q[vq]
                sa_ref[g, vq] = q_prev[vq] - sa[vq] * alpha + vbs[vq] * beta
        return carry

    lax.fori_loop(0, tt, step, 0)

    for g, d, b, _, y_ref in groups:
        flush_y(g, b, y_ref, tt - 1 if d == 0 else 0)

    @pl.when(i == pl.num_programs(0) - 1)
    def _():
        st_ref[...] = s_ref[...]


def _lane_group_expander():
    rows = np.arange(LANES)
    cols = np.arange(JQ * LANES)
    e = ((rows[:, None] // RWKV_HEADS == cols[None, :] // LANES)
         & (rows[:, None] % RWKV_HEADS == cols[None, :] % RWKV_HEADS))
    return np.tile(e.astype(np.float32), (3, 1))


LANE_GROUP_EXPANDER = _lane_group_expander()


def _scan(fwd_streams, bwd_streams, s0, tt=None):
    b, l, _ = fwd_streams[0].shape
    tt = min(l, 64) if tt is None else tt
    assert tt >= 4 and l % tt == 0
    nt = l // tt
    expander = jnp.asarray(LANE_GROUP_EXPANDER, BF16)
    acc_shape = (2 * b, JQ, SUBLANES, LANES)
    tile = lambda a: a.reshape(b, l, SUBLANES, LANES)
    f_spec = pl.BlockSpec((b, tt, SUBLANES, LANES), lambda i: (0, i, 0, 0))
    b_spec = pl.BlockSpec((b, tt, SUBLANES, LANES), lambda i: (0, nt - 1 - i, 0, 0))
    s_shape = (2 * b, JQ, SUBLANES, SUBLANES, LANES)
    s_spec = pl.BlockSpec(s_shape, lambda i: (0, 0, 0, 0, 0))
    y_shape = jax.ShapeDtypeStruct((b, l, SUBLANES, LANES), F32)
    yf, yb, st = pl.pallas_call(
        functools.partial(_scan_kernel, b, tt),
        grid=(nt,),
        in_specs=[f_spec] * 6 + [b_spec] * 6
                 + [pl.BlockSpec(LANE_GROUP_EXPANDER.shape, lambda i: (0, 0)), s_spec],
        out_specs=[f_spec, b_spec, s_spec],
        out_shape=[y_shape, y_shape, jax.ShapeDtypeStruct(s_shape, F32)],
        scratch_shapes=[pltpu.VMEM(s_shape, F32),
                        pltpu.VMEM(acc_shape, F32),
                        pltpu.VMEM(acc_shape, F32),
                        pltpu.VMEM(acc_shape, F32),
                        pltpu.VMEM((2 * b, tt, SUBLANES, JQ * LANES), F32),
                        pltpu.VMEM((2 * b, SUBLANES, LANES), F32)],
        compiler_params=_cparams(("arbitrary",)),
        name="rwkv_scan",
    )(*[tile(a) for a in fwd_streams], *[tile(a) for a in bwd_streams], expander, s0)
    return yf.reshape(b, l, RWKV_WIDTH), yb.reshape(b, l, RWKV_WIDTH), st


CONV_PAD = 16


def _mix_kernel(seg, yf_ref, yb_ref, bf_ref, bb_ref, gate_ref, glu_ref, glat_ref,
                lng_ref, lnb_ref, g2_ref, wro_ref, cw_ref, cb_ref, clg_ref, clb_ref, wco_ref,
                o_ref, upad_ref):
    tm = yf_ref.shape[0]
    nseg = tm // seg
    y = yf_ref[...] + yb_ref[...]
    mu = _headsum(y) * (1.0 / RWKV_HEAD)
    yc = y - mu
    var = _headsum(yc * yc) * (1.0 / RWKV_HEAD)
    yn = yc * lax.rsqrt(var + GN_EPS) * lng_ref[...] + lnb_ref[...]
    yn = yn + bf_ref[...] + bb_ref[...]
    g = jnp.dot(_sigmoid(glat_ref[...]).astype(BF16), g2_ref[...], preferred_element_type=F32)
    b_r = jnp.dot((yn * g).astype(BF16), wro_ref[...], preferred_element_type=F32)
    u = glu_ref[:, 0:CONV_WIDTH] * _sigmoid(glu_ref[:, CONV_WIDTH:])
    zpad = jnp.zeros((CONV_PAD, CONV_WIDTH), F32)
    for s in range(nseg):
        upad_ref[s, 0:CONV_PAD, :] = zpad
        upad_ref[s, CONV_PAD:CONV_PAD + seg, :] = u[s * seg:(s + 1) * seg, :]
        upad_ref[s, CONV_PAD + seg:, :] = zpad
    half = CONV_K // 2
    convs = []
    for s in range(nseg):
        acc = None
        for j in range(CONV_K):
            off = CONV_PAD - half + j
            term = upad_ref[s, off:off + seg, :] * cw_ref[j:j + 1, :]
            acc = term if acc is None else acc + term
        convs.append(acc)
    uc = (convs[0] if nseg == 1 else jnp.concatenate(convs, axis=0)) + cb_ref[...]
    m1 = jnp.mean(uc, axis=-1, keepdims=True)
    ucc = uc - m1
    v1 = jnp.mean(ucc * ucc, axis=-1, keepdims=True)
    un = ucc * lax.rsqrt(v1 + LN_EPS) * clg_ref[...] + clb_ref[...]
    b_c = jnp.dot((un * _sigmoid(un)).astype(BF16), wco_ref[...], preferred_element_type=F32)
    g_r = _sigmoid(gate_ref[:, 0:D_MODEL])
    g_c = _sigmoid(gate_ref[:, D_MODEL:])
    o_ref[...] = (g_r * b_r + g_c * b_c).astype(BF16)


def _mix(seg, yf, yb, bonus_f, bonus_b, p, lnx_g, lnx_b, g2, w_ro, conv_w, conv_b, cln_g, cln_b, w_co):
    b, l, _ = yf.shape
    tm = min(l, 256)
    assert tm % seg == 0
    tok = lambda n, blk=0: pl.BlockSpec((None, tm, n), lambda bi, i, blk=blk: (bi, i, blk))
    full = lambda a: pl.BlockSpec(a.shape, lambda bi, i: (0,) * a.ndim)
    params = (lnx_g, lnx_b, g2, w_ro, conv_w, conv_b, cln_g, cln_b, w_co)
    return pl.pallas_call(
        functools.partial(_mix_kernel, seg),
        grid=(b, l // tm),
        in_specs=[tok(RWKV_WIDTH)] * 4 + [tok(2 * D_MODEL, PC_GATE // (2 * D_MODEL)),
                                          tok(2 * CONV_WIDTH, PC_GLU // (2 * CONV_WIDTH)),
                                          tok(GATE_LORA, PC_GLAT // GATE_LORA)]
                 + [full(a) for a in params],
        out_specs=tok(D_MODEL),
        out_shape=jax.ShapeDtypeStruct((b, l, D_MODEL), BF16),
        scratch_shapes=[pltpu.VMEM((tm // seg, seg + 2 * CONV_PAD, CONV_WIDTH), F32)],
        compiler_params=_cparams(("parallel", "parallel")),
        name="mixer_branches",
    )(yf, yb, bonus_f, bonus_b, p, p, p, *params)


def _out_kernel(m_ref, w_ref, x_ref, gate_ref, o_ref):
    o_ref[...] = x_ref[...] + gate_ref[...] * jnp.dot(m_ref[...], w_ref[...], preferred_element_type=F32)


def _out_proj(m, w_o, x, gate):
    b, l, d = x.shape
    tm = min(l, 512)
    tok = pl.BlockSpec((None, tm, d), lambda bi, i: (bi, i, 0))
    return pl.pallas_call(
        _out_kernel,
        grid=(b, l // tm),
        in_specs=[tok, pl.BlockSpec((d, d), lambda bi, i: (0, 0)), tok,
                  pl.BlockSpec((None, 1, d), lambda bi, i: (bi, 0, 0))],
        out_specs=tok,
        out_shape=jax.ShapeDtypeStruct((b, l, d), F32),
        compiler_params=_cparams(("parallel", "parallel")),
        name="mixer_out",
    )(m, w_o, x, gate)


def _route_kernel(x_ref, g_ref, sh_ref, sc_ref, wr_ref, hn_ref, aff_ref):
    h = _norm_mod(x_ref[...], g_ref[...], sh_ref[...], sc_ref[...])
    hn_ref[...] = h.astype(BF16)
    logits = lax.dot_general(wr_ref[...], h, (((1,), (1,)), ((), ())), precision=HIGHEST,
                             preferred_element_type=F32)
    m = jnp.max(logits, axis=0, keepdims=True)
    e = jnp.exp(logits - m)
    aff_ref[...] = e / jnp.sum(e, axis=0, keepdims=True)


def _route(x, g, shift, scale, w_router_t):
    b, l, d = x.shape
    tm = min(l, 512)
    row = pl.BlockSpec((None, 1, d), lambda bi, i: (bi, 0, 0))
    return pl.pallas_call(
        _route_kernel,
        grid=(b, l // tm),
        in_specs=[pl.BlockSpec((None, tm, d), lambda bi, i: (bi, i, 0)),
                  pl.BlockSpec((1, d), lambda bi, i: (0, 0)), row, row,
                  pl.BlockSpec((N_EXPERTS, d), lambda bi, i: (0, 0))],
        out_specs=[pl.BlockSpec((None, tm, d), lambda bi, i: (bi, i, 0)),
                   pl.BlockSpec((None, N_EXPERTS, tm), lambda bi, i: (bi, 0, i))],
        out_shape=[jax.ShapeDtypeStruct((b, l, d), BF16), jax.ShapeDtypeStruct((b, N_EXPERTS, l), F32)],
        compiler_params=_cparams(("parallel", "parallel")),
        name="moe_route",
    )(x, g, shift, scale, w_router_t)


SEL_BLK = LANES


def _select_kernel(cap, aff_ref, slot_ref, off_ref):
    ne, l = aff_ref.shape
    nblk = l // SEL_BLK

    def bits():
        return pltpu.bitcast(aff_ref[...], I32)

    def count(mask_i32):
        return jnp.sum(mask_i32, axis=1, keepdims=True)

    def value_step(it, prefix):
        cand = prefix | lax.shift_left(jnp.int32(1), 30 - it)
        cnt = count(jnp.where(bits() >= cand, 1, 0))
        return jnp.where(cnt >= cap, cand, prefix)

    tau = lax.fori_loop(0, 31, value_step, jnp.zeros((ne, 1), I32))
    need = cap - count(jnp.where(bits() > tau, 1, 0))
    tok = lax.broadcasted_iota(I32, (ne, l), 1)
    nbits = int(l).bit_length()

    def index_step(it, tp):
        cand = tp | lax.shift_left(jnp.int32(1), nbits - 1 - it)
        f = count(jnp.where((bits() == tau) & (tok < cand), 1, 0))
        return jnp.where(f < need, cand, tp)

    tlast = lax.fori_loop(0, nbits, index_step, jnp.zeros((ne, 1), I32))
    bt = bits()
    sel = (bt > tau) | ((bt == tau) & (tok <= tlast))
    sel_b = jnp.where(sel, 1.0, 0.0).astype(BF16)
    blk_of = lax.broadcasted_iota(I32, (l, nblk), 0) // SEL_BLK
    blk_id = lax.broadcasted_iota(I32, (l, nblk), 1)
    tot = jnp.dot(sel_b, jnp.where(blk_of == blk_id, 1.0, 0.0).astype(BF16), preferred_element_type=F32)
    ra = lax.broadcasted_iota(I32, (nblk, nblk), 0)
    ca = lax.broadcasted_iota(I32, (nblk, nblk), 1)
    offs = jnp.dot(tot.astype(BF16), jnp.where(ra < ca, 1.0, 0.0).astype(BF16), preferred_element_type=F32)
    off_ref[...] = offs.astype(I32)
    ri = lax.broadcasted_iota(I32, (SEL_BLK, SEL_BLK), 0)
    ci = lax.broadcasted_iota(I32, (SEL_BLK, SEL_BLK), 1)
    strict_upper = jnp.where(ri < ci, 1.0, 0.0).astype(BF16)
    for c in range(nblk):
        sl = slice(c * SEL_BLK, (c + 1) * SEL_BLK)
        excl = jnp.dot(sel_b[:, sl], strict_upper, preferred_element_type=F32)
        pos = (excl + offs[:, c:c + 1]).astype(I32)
        slot_ref[:, sl] = jnp.where(sel[:, sl], pos, -1)


def _select(aff, cap):
    b, ne, l = aff.shape
    nblk = l // SEL_BLK
    return pl.pallas_call(
        functools.partial(_select_kernel, cap),
        grid=(b,),
        in_specs=[pl.BlockSpec((None, ne, l), lambda bi: (bi, 0, 0))],
        out_specs=[pl.BlockSpec((None, ne, l), lambda bi: (bi, 0, 0)),
                   pl.BlockSpec((None, ne, nblk), lambda bi: (bi, 0, 0))],
        out_shape=[jax.ShapeDtypeStruct((b, ne, l), I32), jax.ShapeDtypeStruct((b, ne, nblk), I32)],
        compiler_params=_cparams(("parallel",)),
        name="moe_select",
    )(aff)


GATHER_TS = 256
GATHER_WIN = GATHER_TS + BF16_ROWS
COMBINE_TS = SEL_BLK
COMBINE_WIN = COMBINE_TS + BF16_ROWS


def _expert_kernel(cap, nblk, tb, off_ref, slot_ref, hn_ref, wg_ref, wu_ref, wd_ref, y_ref, xs_ref):
    e, b, n = pl.program_id(0), pl.program_id(1), pl.program_id(2)

    @pl.when(n == 0)
    def _():
        xs_ref[...] = jnp.zeros(xs_ref.shape, xs_ref.dtype)

    base = (b * N_EXPERTS + e) * nblk + n * (tb // SEL_BLK)
    srow = lax.broadcasted_iota(I32, (GATHER_WIN, GATHER_TS), 0)
    for j in range(tb // GATHER_TS):
        p0 = off_ref[base + j * (GATHER_TS // SEL_BLK)]
        start = pl.multiple_of((p0 // BF16_ROWS) * BF16_ROWS, BF16_ROWS)
        rel = slot_ref[:, j * GATHER_TS:(j + 1) * GATHER_TS] - start
        onehot = jnp.where(srow == rel, 1.0, 0.0).astype(BF16)
        rows = jnp.dot(onehot, hn_ref[j * GATHER_TS:(j + 1) * GATHER_TS, :], preferred_element_type=F32)
        xs_ref[pl.ds(start, GATHER_WIN), :] = xs_ref[pl.ds(start, GATHER_WIN), :] + rows.astype(BF16)

    @pl.when(n == pl.num_programs(2) - 1)
    def _():
        xs = xs_ref[0:cap, :]
        gh = jnp.dot(xs, wg_ref[...], preferred_element_type=F32)
        uh = jnp.dot(xs, wu_ref[...], preferred_element_type=F32)
        act = (gh * _sigmoid(gh)) * uh
        y_ref[0:cap, :] = jnp.dot(act.astype(BF16), wd_ref[...], preferred_element_type=F32).astype(BF16)
        y_ref[cap:, :] = jnp.zeros((y_ref.shape[0] - cap, y_ref.shape[1]), BF16)


def _experts(off_flat, slot4, hn, wg, wu, wd, cap):
    b, l, d = hn.shape
    nblk = l // SEL_BLK
    tb = min(l, 1024)
    yr = cap + COMBINE_WIN
    grid_spec = pltpu.PrefetchScalarGridSpec(
        num_scalar_prefetch=1,
        grid=(N_EXPERTS, b, l // tb),
        in_specs=[pl.BlockSpec((None, None, 1, tb), lambda e, bi, n, off: (bi, e, 0, n)),
                  pl.BlockSpec((None, tb, d), lambda e, bi, n, off: (bi, n, 0)),
                  pl.BlockSpec((None, d, EXPERT_FF), lambda e, bi, n, off: (e, 0, 0)),
                  pl.BlockSpec((None, d, EXPERT_FF), lambda e, bi, n, off: (e, 0, 0)),
                  pl.BlockSpec((None, EXPERT_FF, d), lambda e, bi, n, off: (e, 0, 0))],
        out_specs=pl.BlockSpec((None, None, yr, d), lambda e, bi, n, off: (bi, e, 0, 0)),
        scratch_shapes=[pltpu.VMEM((cap + GATHER_WIN, d), BF16)])
    return pl.pallas_call(
        functools.partial(_expert_kernel, cap, nblk, tb),
        grid_spec=grid_spec,
        out_shape=jax.ShapeDtypeStruct((b, N_EXPERTS, yr, d), BF16),
        compiler_params=_cparams(("parallel", "parallel", "arbitrary")),
        name="moe_experts",
    )(off_flat, slot4, hn, wg, wu, wd)


def _combine_kernel(nblk, tb, off_ref, slot_ref, aff_ref, y_ref, x_ref, gate_ref, o_ref, acc_ref):
    b, n, e = pl.program_id(0), pl.program_id(1), pl.program_id(2)

    @pl.when(e == 0)
    def _():
        acc_ref[...] = jnp.zeros(acc_ref.shape, F32)

    base = (b * N_EXPERTS + e) * nblk + n * (tb // SEL_BLK)
    srow = lax.broadcasted_iota(I32, (COMBINE_WIN, COMBINE_TS), 0)
    for j in range(tb // COMBINE_TS):
        sl = slice(j * COMBINE_TS, (j + 1) * COMBINE_TS)
        p0 = off_ref[base + j]
        start = pl.multiple_of((p0 // BF16_ROWS) * BF16_ROWS, BF16_ROWS)
        rel = slot_ref[:, sl] - start
        wt = jnp.where(srow == rel, aff_ref[:, sl], 0.0).astype(BF16)
        contrib = lax.dot_general(wt, y_ref[pl.ds(start, COMBINE_WIN), :], (((0,), (0,)), ((), ())),
                                  preferred_element_type=F32)
        acc_ref[sl, :] = acc_ref[sl, :] + contrib

    @pl.when(e == pl.num_programs(2) - 1)
    def _():
        o_ref[...] = x_ref[...] + gate_ref[...] * acc_ref[...]


def _combine(off_flat, slot4, aff4, y, x, gate):
    b, l, d = x.shape
    nblk = l // SEL_BLK
    tb = min(l, 1024)
    yr = y.shape[2]
    tok1 = pl.BlockSpec((None, None, 1, tb), lambda bi, n, e, off: (bi, e, 0, n))
    tokd = pl.BlockSpec((None, tb, d), lambda bi, n, e, off: (bi, n, 0))
    grid_spec = pltpu.PrefetchScalarGridSpec(
        num_scalar_prefetch=1,
        grid=(b, l // tb, N_EXPERTS),
        in_specs=[tok1, tok1,
                  pl.BlockSpec((None, None, yr, d), lambda bi, n, e, off: (bi, e, 0, 0)),
                  tokd, pl.BlockSpec((None, 1, d), lambda bi, n, e, off: (bi, 0, 0))],
        out_specs=tokd,
        scratch_shapes=[pltpu.VMEM((tb, d), F32)])
    return pl.pallas_call(
        functools.partial(_combine_kernel, nblk, tb),
        grid_spec=grid_spec,
        out_shape=jax.ShapeDtypeStruct((b, l, d), F32),
        compiler_params=_cparams(("parallel", "parallel", "arbitrary")),
        name="moe_combine",
    )(off_flat, slot4, aff4, y, x, gate)


def _final_kernel(x_ref, g_ref, o_ref):
    x = x_ref[...]
    ms = jnp.mean(x * x, axis=-1, keepdims=True)
    o_ref[...] = x * lax.rsqrt(ms + RMS_EPS) * g_ref[...]


def _final_norm(x, g):
    b, l, d = x.shape
    tm = min(l, 512)
    tok = pl.BlockSpec((None, tm, d), lambda bi, i: (bi, i, 0))
    return pl.pallas_call(
        _final_kernel,
        grid=(b, l // tm),
        in_specs=[tok, pl.BlockSpec((1, d), lambda bi, i: (0, 0))],
        out_specs=tok,
        out_shape=jax.ShapeDtypeStruct((b, l, d), F32),
        compiler_params=_cparams(("parallel", "parallel")),
        name="final_norm",
    )(x, g)


def _rwkv_streams(p, lw):
    outs = []
    for d in range(2):
        outs.append(_prep(d, p, lw['mu_rkv'][d], lw['mu_wa'][d], lw['w0'][d], lw['a0'][d], lw['k_k'][d],
                          lw['k_a'][d], lw['r_k'], lw['w_lora'][d]))
    return outs


def _mixer(x, mods, lw, s0, seg, tail=True):
    p = _in_proj(x, lw['norm1_g'], mods[0], mods[1], lw['w_in'])
    sf, sb = _rwkv_streams(p, lw)
    yf, yb, st = _scan(sf[:6], sb[:6], s0)
    if not tail:
        return None, st
    m = _mix(seg, yf, yb, sf[6], sb[6], p, lw['lnx_g'], lw['lnx_b'], lw['g2'], lw['w_rwkv_out'],
             lw['conv_w'], lw['conv_b'], lw['conv_ln_g'], lw['conv_ln_b'], lw['w_conv_out'])
    return _out_proj(m, lw['w_o'], x, mods[2]), st


def _moe(x, mods, lw):
    b, l, d = x.shape
    cap = EC_CAPACITY * l // N_EXPERTS
    hn, aff = _route(x, lw['norm2_g'], mods[3], mods[4], lw['w_router_t'])
    slot, off = _select(aff, cap)
    slot4 = slot.reshape(b, N_EXPERTS, 1, l)
    aff4 = aff.reshape(b, N_EXPERTS, 1, l)
    off_flat = off.reshape(-1)
    y = _experts(off_flat, slot4, hn, lw['w_exp_gate'], lw['w_exp_up'], lw['w_exp_down'], cap)
    return _combine(off_flat, slot4, aff4, y, x, mods[5])


def _perm_last(a):
    lead = a.shape[:-1]
    a = a.reshape(lead + (RWKV_HEADS, SUBLANES, JQ))
    return jnp.moveaxis(a, -3, -1).reshape(lead + (RWKV_WIDTH,))


def _layer_weights(l, w):
    row = lambda a: a.reshape(1, -1)
    mu = w['tok_mu'][l]
    mu_rkv = [row(jnp.concatenate([_perm_last(mu[d, i * RWKV_WIDTH:(i + 1) * RWKV_WIDTH]) for i in range(3)]))
              for d in range(2)]
    mu_wa = [row(mu[d, 3 * RWKV_WIDTH:]) for d in range(2)]
    zeros = jnp.zeros((DECAY_LORA, RWKV_WIDTH), F32)
    w_lora = [jnp.concatenate([jnp.concatenate([_perm_last(w['w2'][l, d]), zeros], axis=1),
                               jnp.concatenate([zeros, _perm_last(w['a2'][l, d])], axis=1)], axis=0).astype(BF16)
              for d in range(2)]
    pv = lambda a: [row(_perm_last(a[l, d])) for d in range(2)]
    wi = w['w_in'][l]
    w_in = jnp.concatenate(
        [wi[:, OFF_GATE:], wi[:, OFF_GLU:OFF_GATE]]
        + [_perm_last(wi[:, i * RWKV_WIDTH:(i + 1) * RWKV_WIDTH]) for i in range(3)]
        + [wi[:, OFF_WA_F:OFF_GLU]], axis=1).astype(BF16)
    return dict(
        norm1_g=row(w['norm1_g'][l]),
        w_in=w_in,
        mu_rkv=mu_rkv, mu_wa=mu_wa, w_lora=w_lora,
        w0=pv(w['w0']), a0=pv(w['a0']), k_k=pv(w['k_k']), k_a=pv(w['k_a']),
        r_k=row(_perm_last(w['r_k'][l])), lnx_g=row(_perm_last(w['lnx_g'][l])),
        lnx_b=row(_perm_last(w['lnx_b'][l])),
        g2=_perm_last(w['g2'][l]).astype(BF16),
        w_rwkv_out=_perm_last(w['w_rwkv_out'][l].T).T.astype(BF16),
        conv_w=w['conv_w'][l], conv_b=row(w['conv_b'][l]), conv_ln_g=row(w['conv_ln_g'][l]),
        conv_ln_b=row(w['conv_ln_b'][l]), w_conv_out=w['w_conv_out'][l].astype(BF16),
        w_o=w['w_o'][l].astype(BF16), norm2_g=row(w['norm2_g'][l]),
        w_router_t=w['w_router'][l].T,
        w_exp_gate=w['w_exp_gate'][l].astype(BF16), w_exp_up=w['w_exp_up'][l].astype(BF16),
        w_exp_down=w['w_exp_down'][l].astype(BF16))


def kernel(x, c, ctx, c_ctx, w_mod, b_mod, norm1_g, w_in, tok_mu, w0, w2, a0, a2, k_k, k_a, r_k, g2, lnx_g,
           lnx_b, w_rwkv_out, conv_w, conv_b, conv_ln_g, conv_ln_b, w_conv_out, w_o, norm2_g, w_router,
           w_exp_gate, w_exp_up, w_exp_down, final_g):
    w = dict(norm1_g=norm1_g, w_in=w_in, tok_mu=tok_mu, w0=w0, w2=w2, a0=a0, a2=a2, k_k=k_k, k_a=k_a, r_k=r_k,
             g2=g2, lnx_g=lnx_g, lnx_b=lnx_b, w_rwkv_out=w_rwkv_out, conv_w=conv_w, conv_b=conv_b,
             conv_ln_g=conv_ln_g, conv_ln_b=conv_ln_b, w_conv_out=w_conv_out, w_o=w_o, norm2_g=norm2_g,
             w_router=w_router, w_exp_gate=w_exp_gate, w_exp_up=w_exp_up, w_exp_down=w_exp_down)
    batch = x.shape[0]
    depth = w_mod.shape[0]
    ctx_len = ctx.shape[1]
    pad_rows = (-(batch + 1)) % SUBLANES
    c_rows = jnp.concatenate([c, c_ctx[None, :], jnp.zeros((pad_rows, D_MODEL), F32)], axis=0)
    mod = _modulation(c_rows, w_mod, b_mod)
    zero_state = jnp.zeros((2 * batch, JQ, SUBLANES, SUBLANES, LANES), F32)
    for l in range(depth):
        lw = _layer_weights(l, w)
        m_lat = [mod[l, 0:batch, i * D_MODEL:(i + 1) * D_MODEL][:, None, :] for i in range(N_MOD)]
        m_ctx = [jnp.broadcast_to(mod[l, batch:batch + 1, i * D_MODEL:(i + 1) * D_MODEL][:, None, :],
                                  (batch, 1, D_MODEL)) for i in range(N_MOD)]
        last = l == depth - 1
        ctx_mid, state = _mixer(ctx, m_ctx, lw, zero_state, ctx_len, tail=not last)
        x, _ = _mixer(x, m_lat, lw, state, GRID_W)
        x = _moe(x, m_lat, lw)
        if not last:
            ctx = _moe(ctx_mid, m_ctx, lw)
    return _final_norm(x, final_g.reshape(1, -1))
```

```python
import functools

import numpy as np
import jax
import jax.numpy as jnp
from jax import lax
from jax.experimental import pallas as pl
from jax.experimental.pallas import tpu as pltpu

F32 = jnp.float32
BF16 = jnp.bfloat16
I32 = jnp.int32
HIGHEST = lax.Precision.HIGHEST

D_MODEL = 2048
GRID_W = 64
RWKV_WIDTH = 1024
RWKV_HEAD = 64
RWKV_HEADS = RWKV_WIDTH // RWKV_HEAD
DECAY_LORA = 64
AAA_LORA = 64
GATE_LORA = 128
GN_EPS = 64e-5
CONV_WIDTH = 1024
CONV_K = 31
LN_EPS = 1e-5
N_EXPERTS = 16
EXPERT_FF = 1024
EC_CAPACITY = 2
N_MOD = 6
RMS_EPS = 1e-6

OFF_WA_F = 3 * RWKV_WIDTH
OFF_WA_B = OFF_WA_F + DECAY_LORA + AAA_LORA
OFF_GLAT = OFF_WA_B + DECAY_LORA + AAA_LORA
OFF_GLU = OFF_GLAT + GATE_LORA
OFF_GATE = OFF_GLU + 2 * CONV_WIDTH
IN_COLS = OFF_GATE + 2 * D_MODEL

LANES = 128
SUBLANES = 8
BF16_ROWS = 16
VMEM_LIMIT_BYTES = 56 * 1024 * 1024

LORA_W = DECAY_LORA + AAA_LORA
PC_GATE = 0
PC_GLU = 2 * D_MODEL
PC_RKV = PC_GLU + 2 * CONV_WIDTH
PC_WA = PC_RKV + 3 * RWKV_WIDTH
PC_GLAT = PC_WA + 2 * LORA_W
assert PC_GLAT + GATE_LORA == IN_COLS
assert PC_GLU % (2 * CONV_WIDTH) == 0 and PC_RKV % (3 * RWKV_WIDTH) == 0 and PC_WA % LORA_W == 0

JQ = RWKV_HEAD // SUBLANES
assert JQ * RWKV_HEADS == LANES


def _channel_perm():
    p = np.arange(RWKV_WIDTH)
    j2, jq, h = p // LANES, (p % LANES) // RWKV_HEADS, p % RWKV_HEADS
    return (h * RWKV_HEAD + j2 * JQ + jq).astype(np.int32)


PERM = _channel_perm()


def _in_col_order():
    return np.concatenate([
        OFF_GATE + np.arange(2 * D_MODEL), OFF_GLU + np.arange(2 * CONV_WIDTH),
        PERM, RWKV_WIDTH + PERM, 2 * RWKV_WIDTH + PERM,
        OFF_WA_F + np.arange(LORA_W), OFF_WA_B + np.arange(LORA_W),
        OFF_GLAT + np.arange(GATE_LORA)]).astype(np.int32)


IN_COL_ORDER = _in_col_order()


def _cparams(sem, flags=None):
    return pltpu.CompilerParams(dimension_semantics=sem, vmem_limit_bytes=VMEM_LIMIT_BYTES, flags=flags)


def _sigmoid(x):
    return 1.0 / (1.0 + jnp.exp(-x))


def _softplus(x):
    return jnp.maximum(x, 0.0) + jnp.log1p(jnp.exp(-jnp.abs(x)))


def _group_allreduce(x):
    x = x + pltpu.roll(x, RWKV_HEADS, 1)
    x = x + pltpu.roll(x, 2 * RWKV_HEADS, 1)
    return x + pltpu.roll(x, 4 * RWKV_HEADS, 1)


def _headsum(x):
    s = x[:, 0:LANES]
    for i in range(1, SUBLANES):
        s = s + x[:, i * LANES:(i + 1) * LANES]
    s = _group_allreduce(s)
    return jnp.concatenate([s] * SUBLANES, axis=1)


def _mod_kernel(c_ref, w_ref, b_ref, o_ref):
    c = c_ref[...]
    s = c * _sigmoid(c)
    o_ref[...] = jnp.dot(s, w_ref[...], precision=HIGHEST, preferred_element_type=F32) + b_ref[...]


def _modulation(c_rows, w_mod, b_mod):
    depth, d, n = w_mod.shape
    rows = c_rows.shape[0]
    tn = 1024
    return pl.pallas_call(
        _mod_kernel,
        grid=(depth, n // tn),
        in_specs=[pl.BlockSpec((rows, d), lambda l, j: (0, 0)),
                  pl.BlockSpec((None, d, tn), lambda l, j: (l, 0, j)),
                  pl.BlockSpec((None, 1, tn), lambda l, j: (l, 0, j))],
        out_specs=pl.BlockSpec((None, rows, tn), lambda l, j: (l, 0, j)),
        out_shape=jax.ShapeDtypeStruct((depth, rows, n), F32),
        compiler_params=_cparams(("parallel", "parallel")),
        name="modulation",
    )(c_rows, w_mod, b_mod.reshape(depth, 1, n))


def _norm_mod(x, g, shift, scale):
    ms = jnp.mean(x * x, axis=-1, keepdims=True)
    h = x * lax.rsqrt(ms + RMS_EPS) * g
    return h * (1.0 + scale) + shift


def _in_kernel(x_ref, g_ref, sh_ref, sc_ref, w_ref, o_ref):
    h = _norm_mod(x_ref[...], g_ref[...], sh_ref[...], sc_ref[...])
    o_ref[...] = jnp.dot(h.astype(BF16), w_ref[...], preferred_element_type=F32)


def _in_proj(x, g, shift, scale, w_bf16):
    b, l, d = x.shape
    n = w_bf16.shape[1]
    tm = min(l, 512)
    tn = 1920
    return pl.pallas_call(
        _in_kernel,
        grid=(n // tn, b, l // tm),
        in_specs=[pl.BlockSpec((None, tm, d), lambda j, bi, i: (bi, i, 0)),
                  pl.BlockSpec((1, d), lambda j, bi, i: (0, 0)),
                  pl.BlockSpec((None, 1, d), lambda j, bi, i: (bi, 0, 0)),
                  pl.BlockSpec((None, 1, d), lambda j, bi, i: (bi, 0, 0)),
                  pl.BlockSpec((d, tn), lambda j, bi, i: (0, j))],
        out_specs=pl.BlockSpec((None, tm, tn), lambda j, bi, i: (bi, i, j)),
        out_shape=jax.ShapeDtypeStruct((b, l, n), F32),
        compiler_params=_cparams(("parallel", "parallel", "parallel")),
        name="in_proj",
    )(x, g, shift, scale, w_bf16)


def _prep_kernel(direction, rkv_ref, rkv_halo_ref, wa_ref, wa_halo_ref, mu_rkv_ref, mu_wa_ref,
                 w0_ref, a0_ref, kk_ref, ka_ref, rk_ref, wl_ref,
                 r_out, k_out, v_out, w_out, kk_out, kka_out, bonus_out):
    i = pl.program_id(1)
    last = pl.num_programs(1) - 1
    tm = rkv_ref.shape[0]
    rows = lax.broadcasted_iota(I32, (tm, 1), 0)

    def shifted(ref, halo_ref):
        z = ref[...]
        if direction == 0:
            halo = halo_ref[SUBLANES - 1:SUBLANES, :] * jnp.where(i == 0, 0.0, 1.0)
            return z, jnp.where(rows == 0, halo, pltpu.roll(z, 1, 0))
        halo = halo_ref[0:1, :] * jnp.where(i == last, 0.0, 1.0)
        return z, jnp.where(rows == tm - 1, halo, pltpu.roll(z, tm - 1, 0))

    z, zs = shifted(rkv_ref, rkv_halo_ref)
    zm = z + mu_rkv_ref[...] * (zs - z)
    r = zm[:, 0:RWKV_WIDTH]
    k = zm[:, RWKV_WIDTH:2 * RWKV_WIDTH]
    v = zm[:, 2 * RWKV_WIDTH:]
    wa, was = shifted(wa_ref, wa_halo_ref)
    wam = wa + mu_wa_ref[...] * (was - wa)
    lane = lax.broadcasted_iota(I32, wam.shape, 1)
    lat = jnp.where(lane < DECAY_LORA, jnp.tanh(wam), wam)
    lo = jnp.dot(lat.astype(BF16), wl_ref[...], preferred_element_type=F32)
    w_log = -_softplus(-(w0_ref[...] + lo[:, 0:RWKV_WIDTH])) - 0.5
    decay = jnp.exp(-jnp.exp(w_log))
    a = _sigmoid(a0_ref[...] + lo[:, RWKV_WIDTH:])
    kk = k * kk_ref[...]
    kk = kk / jnp.maximum(jnp.sqrt(_headsum(kk * kk)), 1e-12)
    k2 = k * (1.0 + (a - 1.0) * ka_ref[...])
    r_out[...] = r
    k_out[...] = k2
    v_out[...] = v
    w_out[...] = decay
    kk_out[...] = kk
    kka_out[...] = kk * a
    bonus_out[...] = _headsum(r * k2 * rk_ref[...]) * v


def _prep(direction, p, mu_rkv, mu_wa, w0, a0, k_k, k_a, r_k, w_lora):
    b, l, _ = p.shape
    tm = min(l, 256)
    nt = l // tm
    hb = tm // SUBLANES
    nh = l // SUBLANES
    if direction == 0:
        halo_idx = lambda i: jnp.maximum(i * hb - 1, 0)
    else:
        halo_idx = lambda i: jnp.minimum((i + 1) * hb, nh - 1)
    rkv_blk = PC_RKV // (3 * RWKV_WIDTH)
    wa_blk = PC_WA // LORA_W + direction
    vec = lambda n: pl.BlockSpec((1, n), lambda bi, i: (0, 0))
    out = jax.ShapeDtypeStruct((b, l, RWKV_WIDTH), F32)
    return pl.pallas_call(
        functools.partial(_prep_kernel, direction),
        grid=(b, nt),
        in_specs=[pl.BlockSpec((None, tm, 3 * RWKV_WIDTH), lambda bi, i: (bi, i, rkv_blk)),
                  pl.BlockSpec((None, SUBLANES, 3 * RWKV_WIDTH), lambda bi, i: (bi, halo_idx(i), rkv_blk)),
                  pl.BlockSpec((None, tm, LORA_W), lambda bi, i: (bi, i, wa_blk)),
                  pl.BlockSpec((None, SUBLANES, LORA_W), lambda bi, i: (bi, halo_idx(i), wa_blk)),
                  vec(3 * RWKV_WIDTH), vec(LORA_W), vec(RWKV_WIDTH), vec(RWKV_WIDTH), vec(RWKV_WIDTH),
                  vec(RWKV_WIDTH), vec(RWKV_WIDTH),
                  pl.BlockSpec((LORA_W, 2 * RWKV_WIDTH), lambda bi, i: (0, 0))],
        out_specs=[pl.BlockSpec((None, tm, RWKV_WIDTH), lambda bi, i: (bi, i, 0))] * 7,
        out_shape=[out] * 7,
        compiler_params=_cparams(("parallel", "parallel")),
        name="rwkv_prep",
    )(p, p, p, p, mu_rkv, mu_wa, w0, a0, k_k, k_a, r_k, w_lora)


def _reduce_scatter_level(cur, lane):
    half = len(cur) // 2
    width = RWKV_HEADS * half
    low = (lane % (2 * width)) < width
    nxt = []
    for j in range(half):
        keep = jnp.where(low, cur[j], cur[j + half])
        other = jnp.where(low, cur[j + half], cur[j])
        if 2 * width == LANES:
            moved = pltpu.roll(other, width, 1)
        else:
            moved = jnp.where(low, pltpu.roll(other, LANES - width, 1), pltpu.roll(other, width, 1))
        nxt.append(keep + moved)
    return nxt


def _scan_kernel(nb, tt, *refs):
    fwd, bwd = refs[0:6], refs[6:12]
    e3_ref, s0_ref, yf_ref, yb_ref, st_ref, s_ref, sa_ref, qr_ref, yr_ref, y1_ref, y2_ref, vb_ref, wkk_ref = refs[12:]
    i = pl.program_id(0)
    lane = lax.broadcasted_iota(I32, (SUBLANES, LANES), 1)
    tile = (SUBLANES, LANES)

    @pl.when(i == 0)
    def _():
        s_ref[...] = s0_ref[...]

    def row(ref, b, t, k2):
        return jnp.broadcast_to(ref[b, pl.ds(t, 1), k2, :], tile)

    def wkk_row(g, k2):
        return jnp.broadcast_to(wkk_ref[g, k2:k2 + 1, :], tile)

    groups = []
    for d, (streams, y_ref) in enumerate(((fwd, yf_ref), (bwd, yb_ref))):
        for b in range(nb):
            groups.append((d * nb + b, d, b, streams, y_ref))

    def advance_y(g, b, y_ref, t_out):
        y_ref[b, t_out] = _reduce_scatter_level([y2_ref[g, j] for j in range(2)], lane)[0]
        lvl2 = _reduce_scatter_level([y1_ref[g, j] for j in range(4)], lane)
        lvl1 = _reduce_scatter_level([yr_ref[g, j] for j in range(JQ)], lane)
        for j in range(2):
            y2_ref[g, j] = lvl2[j]
        for j in range(4):
            y1_ref[g, j] = lvl1[j]

    def head_dot(x, y):
        return _group_allreduce(jnp.broadcast_to(jnp.sum(x * y, axis=0, keepdims=True), tile))

    for g, d, b, streams, _ in groups:
        _, _, v_ref, w_ref, kk_ref, _ = streams
        v = v_ref[b].reshape(tt * SUBLANES, LANES)
        hi = v.astype(BF16)
        r1 = v - hi.astype(F32)
        mid = r1.astype(BF16)
        lo = (r1 - mid.astype(F32)).astype(BF16)
        vb = jnp.dot(jnp.concatenate([hi, mid, lo], axis=1), e3_ref[...], preferred_element_type=F32)
        vb_ref[g] = vb.reshape(tt, SUBLANES, JQ * LANES)
        t0, t1 = (0, 1) if d == 0 else (tt - 1, tt - 2)
        wkk_ref[g] = w_ref[b, t0] * kk_ref[b, t1]
        for vq in range(JQ):
            acc_sa = None
            acc_q = None
            for k2 in range(SUBLANES):
                s = s_ref[g, vq, k2]
                psa = s * row(kk_ref, b, t0, k2)
                pq = s * wkk_row(g, k2)
                acc_sa = psa if acc_sa is None else acc_sa + psa
                acc_q = pq if acc_q is None else acc_q + pq
            sa_ref[g, vq] = _group_allreduce(acc_sa)
            qr_ref[g, vq] = acc_q
            yr_ref[g, vq] = jnp.zeros(tile, F32)
        for j in range(4):
            y1_ref[g, j] = jnp.zeros(tile, F32)
        for j in range(2):
            y2_ref[g, j] = jnp.zeros(tile, F32)

    def step(t, carry):
        for g, d, b, streams, y_ref in groups:
            r_ref, k_ref, v_ref, w_ref, kk_ref, ka_ref = streams
            if d == 0:
                tc = t
                t_out = jnp.maximum(t - 3, 0)
                n1 = jnp.minimum(t + 1, tt - 1)
                n2 = jnp.minimum(t + 2, tt - 1)
            else:
                tc = tt - 1 - t
                t_out = jnp.minimum(tc + 3, tt - 1)
                n1 = jnp.maximum(tc - 1, 0)
                n2 = jnp.maximum(tc - 2, 0)
            advance_y(g, b, y_ref, t_out)
            kk1 = kk_ref[b, n1]
            alpha = head_dot(ka_ref[b, tc], kk1)
            beta = head_dot(k_ref[b, tc], kk1)
            q_prev = [_group_allreduce(qr_ref[g, vq]) for vq in range(JQ)]
            wkk_ref[g] = w_ref[b, n1] * kk_ref[b, n2]
            sa = [sa_ref[g, vq] for vq in range(JQ)]
            vbs = [vb_ref[g, tc, :, vq * LANES:(vq + 1) * LANES] for vq in range(JQ)]
            acc_y = [None] * JQ
            acc_q = [None] * JQ
            for k2 in range(SUBLANES):
                w_r = row(w_ref, b, tc, k2)
                ka_r = row(ka_ref, b, tc, k2)
                k_r = row(k_ref, b, tc, k2)
                r_r = row(r_ref, b, tc, k2)
                wkk_r = wkk_row(g, k2)
                for vq in range(JQ):
                    new = s_ref[g, vq, k2] * w_r - sa[vq] * ka_r + vbs[vq] * k_r
                    s_ref[g, vq, k2] = new
                    py = new * r_r
                    pq = new * wkk_r
                    acc_y[vq] = py if acc_y[vq] is None else acc_y[vq] + py
                    acc_q[vq] = pq if acc_q[vq] is None else acc_q[vq] + pq
            for vq in range(JQ):
                yr_ref[g, vq] = acc_y[vq]
                qr_ref[g, vq] = acc_q[vq]
                sa_ref[g, vq] = q_prev[vq] - sa[vq] * alpha + vbs[vq] * beta
        return carry

    lax.fori_loop(0, tt, step, 0)

    for back in (3, 2, 1):
        for g, d, b, _, y_ref in groups:
            advance_y(g, b, y_ref, tt - back if d == 0 else back - 1)

    @pl.when(i == pl.num_programs(0) - 1)
    def _():
        st_ref[...] = s_ref[...]


def _lane_group_expander():
    rows = np.arange(LANES)
    cols = np.arange(JQ * LANES)
    e = ((rows[:, None] // RWKV_HEADS == cols[None, :] // LANES)
         & (rows[:, None] % RWKV_HEADS == cols[None, :] % RWKV_HEADS))
    return np.tile(e.astype(np.float32), (3, 1))


LANE_GROUP_EXPANDER = _lane_group_expander()


def _scan(fwd_streams, bwd_streams, s0, tt=None):
    b, l, _ = fwd_streams[0].shape
    tt = min(l, 64) if tt is None else tt
    assert tt >= 4 and l % tt == 0
    nt = l // tt
    expander = jnp.asarray(LANE_GROUP_EXPANDER, BF16)
    acc_shape = (2 * b, JQ, SUBLANES, LANES)
    tile = lambda a: a.reshape(b, l, SUBLANES, LANES)
    f_spec = pl.BlockSpec((b, tt, SUBLANES, LANES), lambda i: (0, i, 0, 0))
    b_spec = pl.BlockSpec((b, tt, SUBLANES, LANES), lambda i: (0, nt - 1 - i, 0, 0))
    s_shape = (2 * b, JQ, SUBLANES, SUBLANES, LANES)
    s_spec = pl.BlockSpec(s_shape, lambda i: (0, 0, 0, 0, 0))
    y_shape = jax.ShapeDtypeStruct((b, l, SUBLANES, LANES), F32)
    yf, yb, st = pl.pallas_call(
        functools.partial(_scan_kernel, b, tt),
        grid=(nt,),
        in_specs=[f_spec] * 6 + [b_spec] * 6
                 + [pl.BlockSpec(LANE_GROUP_EXPANDER.shape, lambda i: (0, 0)), s_spec],
        out_specs=[f_spec, b_spec, s_spec],
        out_shape=[y_shape, y_shape, jax.ShapeDtypeStruct(s_shape, F32)],
        scratch_shapes=[pltpu.VMEM(s_shape, F32),
                        pltpu.VMEM(acc_shape, F32),
                        pltpu.VMEM(acc_shape, F32),
                        pltpu.VMEM(acc_shape, F32),
                        pltpu.VMEM((2 * b, 4, SUBLANES, LANES), F32),
                        pltpu.VMEM((2 * b, 2, SUBLANES, LANES), F32),
                        pltpu.VMEM((2 * b, tt, SUBLANES, JQ * LANES), F32),
                        pltpu.VMEM((2 * b, SUBLANES, LANES), F32)],
        compiler_params=_cparams(("arbitrary",)),
        name="rwkv_scan",
    )(*[tile(a) for a in fwd_streams], *[tile(a) for a in bwd_streams], expander, s0)
    return yf.reshape(b, l, RWKV_WIDTH), yb.reshape(b, l, RWKV_WIDTH), st


CONV_PAD = 16


def _mix_kernel(seg, yf_ref, yb_ref, bf_ref, bb_ref, gate_ref, glu_ref, glat_ref,
                lng_ref, lnb_ref, g2_ref, wro_ref, cw_ref, cb_ref, clg_ref, clb_ref, wco_ref,
                o_ref, upad_ref, shift_ref):
    tm = yf_ref.shape[0]
    nseg = tm // seg
    y = yf_ref[...] + yb_ref[...]
    mu = _headsum(y) * (1.0 / RWKV_HEAD)
    yc = y - mu
    var = _headsum(yc * yc) * (1.0 / RWKV_HEAD)
    yn = yc * lax.rsqrt(var + GN_EPS) * lng_ref[...] + lnb_ref[...]
    yn = yn + bf_ref[...] + bb_ref[...]
    g = jnp.dot(_sigmoid(glat_ref[...]).astype(BF16), g2_ref[...], preferred_element_type=F32)
    b_r = jnp.dot((yn * g).astype(BF16), wro_ref[...], preferred_element_type=F32)
    u = glu_ref[:, 0:CONV_WIDTH] * _sigmoid(glu_ref[:, CONV_WIDTH:])
    zpad = jnp.zeros((CONV_PAD, CONV_WIDTH), F32)
    for s in range(nseg):
        upad_ref[s, 0:CONV_PAD, :] = zpad
        upad_ref[s, CONV_PAD:CONV_PAD + seg, :] = u[s * seg:(s + 1) * seg, :]
        upad_ref[s, CONV_PAD + seg:, :] = zpad
    first = CONV_PAD - CONV_K // 2
    convs = []
    for s in range(nseg):
        blocks = []
        for c in range(CONV_WIDTH // LANES):
            cols = slice(c * LANES, (c + 1) * LANES)
            acc = None
            for phase in range(SUBLANES):
                taps = [j for j in range(CONV_K) if (first + j) % SUBLANES == phase]
                tiles = [(first + j) // SUBLANES for j in taps]
                span = SUBLANES * max(tiles) + seg
                shift_ref[0:span, :] = upad_ref[s, phase:phase + span, cols]
                for j, q in zip(taps, tiles):
                    term = shift_ref[SUBLANES * q:SUBLANES * q + seg, :] * cw_ref[j:j + 1, cols]
                    acc = term if acc is None else acc + term
            blocks.append(acc)
        convs.append(jnp.concatenate(blocks, axis=1))
    uc = (convs[0] if nseg == 1 else jnp.concatenate(convs, axis=0)) + cb_ref[...]
    m1 = jnp.mean(uc, axis=-1, keepdims=True)
    ucc = uc - m1
    v1 = jnp.mean(ucc * ucc, axis=-1, keepdims=True)
    un = ucc * lax.rsqrt(v1 + LN_EPS) * clg_ref[...] + clb_ref[...]
    b_c = jnp.dot((un * _sigmoid(un)).astype(BF16), wco_ref[...], preferred_element_type=F32)
    g_r = _sigmoid(gate_ref[:, 0:D_MODEL])
    g_c = _sigmoid(gate_ref[:, D_MODEL:])
    o_ref[...] = (g_r * b_r + g_c * b_c).astype(BF16)


def _mix(seg, yf, yb, bonus_f, bonus_b, p, lnx_g, lnx_b, g2, w_ro, conv_w, conv_b, cln_g, cln_b, w_co):
    b, l, _ = yf.shape
    tm = min(l, 256)
    assert tm % seg == 0
    tok = lambda n, blk=0: pl.BlockSpec((None, tm, n), lambda bi, i, blk=blk: (bi, i, blk))
    full = lambda a: pl.BlockSpec(a.shape, lambda bi, i: (0,) * a.ndim)
    params = (lnx_g, lnx_b, g2, w_ro, conv_w, conv_b, cln_g, cln_b, w_co)
    return pl.pallas_call(
        functools.partial(_mix_kernel, seg),
        grid=(b, l // tm),
        in_specs=[tok(RWKV_WIDTH)] * 4 + [tok(2 * D_MODEL, PC_GATE // (2 * D_MODEL)),
                                          tok(2 * CONV_WIDTH, PC_GLU // (2 * CONV_WIDTH)),
                                          tok(GATE_LORA, PC_GLAT // GATE_LORA)]
                 + [full(a) for a in params],
        out_specs=tok(D_MODEL),
        out_shape=jax.ShapeDtypeStruct((b, l, D_MODEL), BF16),
        scratch_shapes=[pltpu.VMEM((tm // seg, seg + 2 * CONV_PAD, CONV_WIDTH), F32),
                        pltpu.VMEM((seg + 2 * CONV_PAD, LANES), F32)],
        compiler_params=_cparams(("parallel", "parallel")),
        name="mixer_branches",
    )(yf, yb, bonus_f, bonus_b, p, p, p, *params)


def _out_kernel(m_ref, w_ref, x_ref, gate_ref, o_ref):
    o_ref[...] = x_ref[...] + gate_ref[...] * jnp.dot(m_ref[...], w_ref[...], preferred_element_type=F32)


def _out_proj(m, w_o, x, gate):
    b, l, d = x.shape
    tm = min(l, 512)
    tok = pl.BlockSpec((None, tm, d), lambda bi, i: (bi, i, 0))
    return pl.pallas_call(
        _out_kernel,
        grid=(b, l // tm),
        in_specs=[tok, pl.BlockSpec((d, d), lambda bi, i: (0, 0)), tok,
                  pl.BlockSpec((None, 1, d), lambda bi, i: (bi, 0, 0))],
        out_specs=tok,
        out_shape=jax.ShapeDtypeStruct((b, l, d), F32),
        compiler_params=_cparams(("parallel", "parallel")),
        name="mixer_out",
    )(m, w_o, x, gate)


def _route_kernel(x_ref, g_ref, sh_ref, sc_ref, wr_ref, hn_ref, aff_ref):
    h = _norm_mod(x_ref[...], g_ref[...], sh_ref[...], sc_ref[...])
    hn_ref[...] = h.astype(BF16)
    logits = lax.dot_general(wr_ref[...], h, (((1,), (1,)), ((), ())), precision=HIGHEST,
                             preferred_element_type=F32)
    m = jnp.max(logits, axis=0, keepdims=True)
    e = jnp.exp(logits - m)
    aff_ref[...] = e / jnp.sum(e, axis=0, keepdims=True)


def _route(x, g, shift, scale, w_router_t):
    b, l, d = x.shape
    tm = min(l, 512)
    row = pl.BlockSpec((None, 1, d), lambda bi, i: (bi, 0, 0))
    return pl.pallas_call(
        _route_kernel,
        grid=(b, l // tm),
        in_specs=[pl.BlockSpec((None, tm, d), lambda bi, i: (bi, i, 0)),
                  pl.BlockSpec((1, d), lambda bi, i: (0, 0)), row, row,
                  pl.BlockSpec((N_EXPERTS, d), lambda bi, i: (0, 0))],
        out_specs=[pl.BlockSpec((None, tm, d), lambda bi, i: (bi, i, 0)),
                   pl.BlockSpec((None, N_EXPERTS, tm), lambda bi, i: (bi, 0, i))],
        out_shape=[jax.ShapeDtypeStruct((b, l, d), BF16), jax.ShapeDtypeStruct((b, N_EXPERTS, l), F32)],
        compiler_params=_cparams(("parallel", "parallel")),
        name="moe_route",
    )(x, g, shift, scale, w_router_t)


SEL_BLK = LANES


def _select_kernel(cap, aff_ref, slot_ref, off_ref):
    ne, l = aff_ref.shape
    nblk = l // SEL_BLK

    def bits():
        return pltpu.bitcast(aff_ref[...], I32)

    def count(mask_i32):
        return jnp.sum(mask_i32, axis=1, keepdims=True)

    def value_step(it, prefix):
        cand = prefix | lax.shift_left(jnp.int32(1), 30 - it)
        cnt = count(jnp.where(bits() >= cand, 1, 0))
        return jnp.where(cnt >= cap, cand, prefix)

    tau = lax.fori_loop(0, 31, value_step, jnp.zeros((ne, 1), I32))
    need = cap - count(jnp.where(bits() > tau, 1, 0))
    tok = lax.broadcasted_iota(I32, (ne, l), 1)
    nbits = int(l).bit_length()

    def index_step(it, tp):
        cand = tp | lax.shift_left(jnp.int32(1), nbits - 1 - it)
        f = count(jnp.where((bits() == tau) & (tok < cand), 1, 0))
        return jnp.where(f < need, cand, tp)

    tlast = lax.fori_loop(0, nbits, index_step, jnp.zeros((ne, 1), I32))
    bt = bits()
    sel = (bt > tau) | ((bt == tau) & (tok <= tlast))
    sel_b = jnp.where(sel, 1.0, 0.0).astype(BF16)
    blk_of = lax.broadcasted_iota(I32, (l, nblk), 0) // SEL_BLK
    blk_id = lax.broadcasted_iota(I32, (l, nblk), 1)
    tot = jnp.dot(sel_b, jnp.where(blk_of == blk_id, 1.0, 0.0).astype(BF16), preferred_element_type=F32)
    ra = lax.broadcasted_iota(I32, (nblk, nblk), 0)
    ca = lax.broadcasted_iota(I32, (nblk, nblk), 1)
    offs = jnp.dot(tot.astype(BF16), jnp.where(ra < ca, 1.0, 0.0).astype(BF16), preferred_element_type=F32)
    off_ref[...] = offs.astype(I32)
    ri = lax.broadcasted_iota(I32, (SEL_BLK, SEL_BLK), 0)
    ci = lax.broadcasted_iota(I32, (SEL_BLK, SEL_BLK), 1)
    strict_upper = jnp.where(ri < ci, 1.0, 0.0).astype(BF16)
    for c in range(nblk):
        sl = slice(c * SEL_BLK, (c + 1) * SEL_BLK)
        excl = jnp.dot(sel_b[:, sl], strict_upper, preferred_element_type=F32)
        pos = (excl + offs[:, c:c + 1]).astype(I32)
        slot_ref[:, sl] = jnp.where(sel[:, sl], pos, -1)


def _select(aff, cap):
    b, ne, l = aff.shape
    nblk = l // SEL_BLK
    return pl.pallas_call(
        functools.partial(_select_kernel, cap),
        grid=(b,),
        in_specs=[pl.BlockSpec((None, ne, l), lambda bi: (bi, 0, 0))],
        out_specs=[pl.BlockSpec((None, ne, l), lambda bi: (bi, 0, 0)),
                   pl.BlockSpec((None, ne, nblk), lambda bi: (bi, 0, 0))],
        out_shape=[jax.ShapeDtypeStruct((b, ne, l), I32), jax.ShapeDtypeStruct((b, ne, nblk), I32)],
        compiler_params=_cparams(("parallel",)),
        name="moe_select",
    )(aff)


GATHER_TS = 256
GATHER_WIN = GATHER_TS + BF16_ROWS
COMBINE_TS = SEL_BLK
COMBINE_WIN = COMBINE_TS + BF16_ROWS


def _expert_kernel(cap, nblk, tb, off_ref, slot_ref, hn_ref, wg_ref, wu_ref, wd_ref, y_ref, xs_ref):
    e, b, n = pl.program_id(0), pl.program_id(1), pl.program_id(2)

    @pl.when(n == 0)
    def _():
        xs_ref[...] = jnp.zeros(xs_ref.shape, xs_ref.dtype)

    base = (b * N_EXPERTS + e) * nblk + n * (tb // SEL_BLK)
    srow = lax.broadcasted_iota(I32, (GATHER_WIN, GATHER_TS), 0)
    for j in range(tb // GATHER_TS):
        p0 = off_ref[base + j * (GATHER_TS // SEL_BLK)]
        start = pl.multiple_of((p0 // BF16_ROWS) * BF16_ROWS, BF16_ROWS)
        rel = slot_ref[:, j * GATHER_TS:(j + 1) * GATHER_TS] - start
        onehot = jnp.where(srow == rel, 1.0, 0.0).astype(BF16)
        rows = jnp.dot(onehot, hn_ref[j * GATHER_TS:(j + 1) * GATHER_TS, :], preferred_element_type=F32)
        xs_ref[pl.ds(start, GATHER_WIN), :] = xs_ref[pl.ds(start, GATHER_WIN), :] + rows.astype(BF16)

    @pl.when(n == pl.num_programs(2) - 1)
    def _():
        xs = xs_ref[0:cap, :]
        gh = jnp.dot(xs, wg_ref[...], preferred_element_type=F32)
        uh = jnp.dot(xs, wu_ref[...], preferred_element_type=F32)
        act = (gh * _sigmoid(gh)) * uh
        y_ref[0:cap, :] = jnp.dot(act.astype(BF16), wd_ref[...], preferred_element_type=F32).astype(BF16)
        y_ref[cap:, :] = jnp.zeros((y_ref.shape[0] - cap, y_ref.shape[1]), BF16)


def _experts(off_flat, slot4, hn, wg, wu, wd, cap):
    b, l, d = hn.shape
    nblk = l // SEL_BLK
    tb = min(l, 1024)
    yr = cap + COMBINE_WIN
    grid_spec = pltpu.PrefetchScalarGridSpec(
        num_scalar_prefetch=1,
        grid=(N_EXPERTS, b, l // tb),
        in_specs=[pl.BlockSpec((None, None, 1, tb), lambda e, bi, n, off: (bi, e, 0, n)),
                  pl.BlockSpec((None, tb, d), lambda e, bi, n, off: (bi, n, 0)),
                  pl.BlockSpec((None, d, EXPERT_FF), lambda e, bi, n, off: (e, 0, 0)),
                  pl.BlockSpec((None, d, EXPERT_FF), lambda e, bi, n, off: (e, 0, 0)),
                  pl.BlockSpec((None, EXPERT_FF, d), lambda e, bi, n, off: (e, 0, 0))],
        out_specs=pl.BlockSpec((None, None, yr, d), lambda e, bi, n, off: (bi, e, 0, 0)),
        scratch_shapes=[pltpu.VMEM((cap + GATHER_WIN, d), BF16)])
    return pl.pallas_call(
        functools.partial(_expert_kernel, cap, nblk, tb),
        grid_spec=grid_spec,
        out_shape=jax.ShapeDtypeStruct((b, N_EXPERTS, yr, d), BF16),
        compiler_params=_cparams(("parallel", "parallel", "arbitrary")),
        name="moe_experts",
    )(off_flat, slot4, hn, wg, wu, wd)


def _combine_kernel(nblk, tb, off_ref, slot_ref, aff_ref, y_ref, x_ref, gate_ref, o_ref, acc_ref):
    b, n, e = pl.program_id(0), pl.program_id(1), pl.program_id(2)

    @pl.when(e == 0)
    def _():
        acc_ref[...] = jnp.zeros(acc_ref.shape, F32)

    base = (b * N_EXPERTS + e) * nblk + n * (tb // SEL_BLK)
    srow = lax.broadcasted_iota(I32, (COMBINE_WIN, COMBINE_TS), 0)
    for j in range(tb // COMBINE_TS):
        sl = slice(j * COMBINE_TS, (j + 1) * COMBINE_TS)
        p0 = off_ref[base + j]
        start = pl.multiple_of((p0 // BF16_ROWS) * BF16_ROWS, BF16_ROWS)
        rel = slot_ref[:, sl] - start
        wt = jnp.where(srow == rel, aff_ref[:, sl], 0.0).astype(BF16)
        contrib = lax.dot_general(wt, y_ref[pl.ds(start, COMBINE_WIN), :], (((0,), (0,)), ((), ())),
                                  preferred_element_type=F32)
        acc_ref[sl, :] = acc_ref[sl, :] + contrib

    @pl.when(e == pl.num_programs(2) - 1)
    def _():
        o_ref[...] = x_ref[...] + gate_ref[...] * acc_ref[...]


def _combine(off_flat, slot4, aff4, y, x, gate):
    b, l, d = x.shape
    nblk = l // SEL_BLK
    tb = min(l, 1024)
    yr = y.shape[2]
    tok1 = pl.BlockSpec((None, None, 1, tb), lambda bi, n, e, off: (bi, e, 0, n))
    tokd = pl.BlockSpec((None, tb, d), lambda bi, n, e, off: (bi, n, 0))
    grid_spec = pltpu.PrefetchScalarGridSpec(
        num_scalar_prefetch=1,
        grid=(b, l // tb, N_EXPERTS),
        in_specs=[tok1, tok1,
                  pl.BlockSpec((None, None, yr, d), lambda bi, n, e, off: (bi, e, 0, 0)),
                  tokd, pl.BlockSpec((None, 1, d), lambda bi, n, e, off: (bi, 0, 0))],
        out_specs=tokd,
        scratch_shapes=[pltpu.VMEM((tb, d), F32)])
    return pl.pallas_call(
        functools.partial(_combine_kernel, nblk, tb),
        grid_spec=grid_spec,
        out_shape=jax.ShapeDtypeStruct((b, l, d), F32),
        compiler_params=_cparams(("parallel", "parallel", "arbitrary")),
        name="moe_combine",
    )(off_flat, slot4, aff4, y, x, gate)


def _final_kernel(x_ref, g_ref, o_ref):
    x = x_ref[...]
    ms = jnp.mean(x * x, axis=-1, keepdims=True)
    o_ref[...] = x * lax.rsqrt(ms + RMS_EPS) * g_ref[...]


def _final_norm(x, g):
    b, l, d = x.shape
    tm = min(l, 512)
    tok = pl.BlockSpec((None, tm, d), lambda bi, i: (bi, i, 0))
    return pl.pallas_call(
        _final_kernel,
        grid=(b, l // tm),
        in_specs=[tok, pl.BlockSpec((1, d), lambda bi, i: (0, 0))],
        out_specs=tok,
        out_shape=jax.ShapeDtypeStruct((b, l, d), F32),
        compiler_params=_cparams(("parallel", "parallel")),
        name="final_norm",
    )(x, g)


def _rwkv_streams(p, lw):
    outs = []
    for d in range(2):
        outs.append(_prep(d, p, lw['mu_rkv'][d], lw['mu_wa'][d], lw['w0'][d], lw['a0'][d], lw['k_k'][d],
                          lw['k_a'][d], lw['r_k'], lw['w_lora'][d]))
    return outs


def _mixer(x, mods, lw, s0, seg, tail=True):
    p = _in_proj(x, lw['norm1_g'], mods[0], mods[1], lw['w_in'])
    sf, sb = _rwkv_streams(p, lw)
    yf, yb, st = _scan(sf[:6], sb[:6], s0)
    if not tail:
        return None, st
    m = _mix(seg, yf, yb, sf[6], sb[6], p, lw['lnx_g'], lw['lnx_b'], lw['g2'], lw['w_rwkv_out'],
             lw['conv_w'], lw['conv_b'], lw['conv_ln_g'], lw['conv_ln_b'], lw['w_conv_out'])
    return _out_proj(m, lw['w_o'], x, mods[2]), st


def _moe(x, mods, lw):
    b, l, d = x.shape
    cap = EC_CAPACITY * l // N_EXPERTS
    hn, aff = _route(x, lw['norm2_g'], mods[3], mods[4], lw['w_router_t'])
    slot, off = _select(aff, cap)
    slot4 = slot.reshape(b, N_EXPERTS, 1, l)
    aff4 = aff.reshape(b, N_EXPERTS, 1, l)
    off_flat = off.reshape(-1)
    y = _experts(off_flat, slot4, hn, lw['w_exp_gate'], lw['w_exp_up'], lw['w_exp_down'], cap)
    return _combine(off_flat, slot4, aff4, y, x, mods[5])


def _perm_last(a):
    lead = a.shape[:-1]
    a = a.reshape(lead + (RWKV_HEADS, SUBLANES, JQ))
    return jnp.moveaxis(a, -3, -1).reshape(lead + (RWKV_WIDTH,))


def _layer_weights(l, w):
    row = lambda a: a.reshape(1, -1)
    mu = w['tok_mu'][l]
    mu_rkv = [row(jnp.concatenate([_perm_last(mu[d, i * RWKV_WIDTH:(i + 1) * RWKV_WIDTH]) for i in range(3)]))
              for d in range(2)]
    mu_wa = [row(mu[d, 3 * RWKV_WIDTH:]) for d in range(2)]
    zeros = jnp.zeros((DECAY_LORA, RWKV_WIDTH), F32)
    w_lora = [jnp.concatenate([jnp.concatenate([_perm_last(w['w2'][l, d]), zeros], axis=1),
                               jnp.concatenate([zeros, _perm_last(w['a2'][l, d])], axis=1)], axis=0).astype(BF16)
              for d in range(2)]
    pv = lambda a: [row(_perm_last(a[l, d])) for d in range(2)]
    wi = w['w_in'][l]
    w_in = jnp.concatenate(
        [wi[:, OFF_GATE:], wi[:, OFF_GLU:OFF_GATE]]
        + [_perm_last(wi[:, i * RWKV_WIDTH:(i + 1) * RWKV_WIDTH]) for i in range(3)]
        + [wi[:, OFF_WA_F:OFF_GLU]], axis=1).astype(BF16)
    return dict(
        norm1_g=row(w['norm1_g'][l]),
        w_in=w_in,
        mu_rkv=mu_rkv, mu_wa=mu_wa, w_lora=w_lora,
        w0=pv(w['w0']), a0=pv(w['a0']), k_k=pv(w['k_k']), k_a=pv(w['k_a']),
        r_k=row(_perm_last(w['r_k'][l])), lnx_g=row(_perm_last(w['lnx_g'][l])),
        lnx_b=row(_perm_last(w['lnx_b'][l])),
        g2=_perm_last(w['g2'][l]).astype(BF16),
        w_rwkv_out=_perm_last(w['w_rwkv_out'][l].T).T.astype(BF16),
        conv_w=w['conv_w'][l], conv_b=row(w['conv_b'][l]), conv_ln_g=row(w['conv_ln_g'][l]),
        conv_ln_b=row(w['conv_ln_b'][l]), w_conv_out=w['w_conv_out'][l].astype(BF16),
        w_o=w['w_o'][l].astype(BF16), norm2_g=row(w['norm2_g'][l]),
        w_router_t=w['w_router'][l].T,
        w_exp_gate=w['w_exp_gate'][l].astype(BF16), w_exp_up=w['w_exp_up'][l].astype(BF16),
        w_exp_down=w['w_exp_down'][l].astype(BF16))


def kernel(x, c, ctx, c_ctx, w_mod, b_mod, norm1_g, w_in, tok_mu, w0, w2, a0, a2, k_k, k_a, r_k, g2, lnx_g,
           lnx_b, w_rwkv_out, conv_w, conv_b, conv_ln_g, conv_ln_b, w_conv_out, w_o, norm2_g, w_router,
           w_exp_gate, w_exp_up, w_exp_down, final_g):
    w = dict(norm1_g=norm1_g, w_in=w_in, tok_mu=tok_mu, w0=w0, w2=w2, a0=a0, a2=a2, k_k=k_k, k_a=k_a, r_k=r_k,
             g2=g2, lnx_g=lnx_g, lnx_b=lnx_b, w_rwkv_out=w_rwkv_out, conv_w=conv_w, conv_b=conv_b,
             conv_ln_g=conv_ln_g, conv_ln_b=conv_ln_b, w_conv_out=w_conv_out, w_o=w_o, norm2_g=norm2_g,
             w_router=w_router, w_exp_gate=w_exp_gate, w_exp_up=w_exp_up, w_exp_down=w_exp_down)
    batch = x.shape[0]
    depth = w_mod.shape[0]
    ctx_len = ctx.shape[1]
    pad_rows = (-(batch + 1)) % SUBLANES
    c_rows = jnp.concatenate([c, c_ctx[None, :], jnp.zeros((pad_rows, D_MODEL), F32)], axis=0)
    mod = _modulation(c_rows, w_mod, b_mod)
    zero_state = jnp.zeros((2 * batch, JQ, SUBLANES, SUBLANES, LANES), F32)
    for l in range(depth):
        lw = _layer_weights(l, w)
        m_lat = [mod[l, 0:batch, i * D_MODEL:(i + 1) * D_MODEL][:, None, :] for i in range(N_MOD)]
        m_ctx = [jnp.broadcast_to(mod[l, batch:batch + 1, i * D_MODEL:(i + 1) * D_MODEL][:, None, :],
                                  (batch, 1, D_MODEL)) for i in range(N_MOD)]
        last = l == depth - 1
        ctx_mid, state = _mixer(ctx, m_ctx, lw, zero_state, ctx_len, tail=not last)
        x, _ = _mixer(x, m_lat, lw, state, GRID_W)
        x = _moe(x, m_lat, lw)
        if not last:
            ctx = _moe(ctx_mid, m_ctx, lw)
    return _final_norm(x, final_g.reshape(1, -1))
```

```python
import functools

import numpy as np
import jax
import jax.numpy as jnp
from jax import lax
from jax.experimental import pallas as pl
from jax.experimental.pallas import tpu as pltpu

F32 = jnp.float32
BF16 = jnp.bfloat16
I32 = jnp.int32
HIGHEST = lax.Precision.HIGHEST

D_MODEL = 2048
GRID_W = 64
RWKV_WIDTH = 1024
RWKV_HEAD = 64
RWKV_HEADS = RWKV_WIDTH // RWKV_HEAD
DECAY_LORA = 64
AAA_LORA = 64
GATE_LORA = 128
GN_EPS = 64e-5
CONV_WIDTH = 1024
CONV_K = 31
LN_EPS = 1e-5
N_EXPERTS = 16
EXPERT_FF = 1024
EC_CAPACITY = 2
N_MOD = 6
RMS_EPS = 1e-6

OFF_WA_F = 3 * RWKV_WIDTH
OFF_WA_B = OFF_WA_F + DECAY_LORA + AAA_LORA
OFF_GLAT = OFF_WA_B + DECAY_LORA + AAA_LORA
OFF_GLU = OFF_GLAT + GATE_LORA
OFF_GATE = OFF_GLU + 2 * CONV_WIDTH
IN_COLS = OFF_GATE + 2 * D_MODEL

LANES = 128
SUBLANES = 8
BF16_ROWS = 16
VMEM_LIMIT_BYTES = 56 * 1024 * 1024

LORA_W = DECAY_LORA + AAA_LORA
PC_GATE = 0
PC_GLU = 2 * D_MODEL
PC_RKV = PC_GLU + 2 * CONV_WIDTH
PC_WA = PC_RKV + 3 * RWKV_WIDTH
PC_GLAT = PC_WA + 2 * LORA_W
assert PC_GLAT + GATE_LORA == IN_COLS
assert PC_GLU % (2 * CONV_WIDTH) == 0 and PC_RKV % (3 * RWKV_WIDTH) == 0 and PC_WA % LORA_W == 0

JQ = RWKV_HEAD // SUBLANES
assert JQ * RWKV_HEADS == LANES


def _channel_perm():
    p = np.arange(RWKV_WIDTH)
    j2, jq, h = p // LANES, (p % LANES) // RWKV_HEADS, p % RWKV_HEADS
    return (h * RWKV_HEAD + j2 * JQ + jq).astype(np.int32)


PERM = _channel_perm()


def _in_col_order():
    return np.concatenate([
        OFF_GATE + np.arange(2 * D_MODEL), OFF_GLU + np.arange(2 * CONV_WIDTH),
        PERM, RWKV_WIDTH + PERM, 2 * RWKV_WIDTH + PERM,
        OFF_WA_F + np.arange(LORA_W), OFF_WA_B + np.arange(LORA_W),
        OFF_GLAT + np.arange(GATE_LORA)]).astype(np.int32)


IN_COL_ORDER = _in_col_order()


def _cparams(sem, flags=None):
    return pltpu.CompilerParams(dimension_semantics=sem, vmem_limit_bytes=VMEM_LIMIT_BYTES, flags=flags)


def _sigmoid(x):
    return 1.0 / (1.0 + jnp.exp(-x))


def _softplus(x):
    return jnp.maximum(x, 0.0) + jnp.log1p(jnp.exp(-jnp.abs(x)))


def _group_allreduce(x):
    x = x + pltpu.roll(x, RWKV_HEADS, 1)
    x = x + pltpu.roll(x, 2 * RWKV_HEADS, 1)
    return x + pltpu.roll(x, 4 * RWKV_HEADS, 1)


def _headsum(x):
    s = x[:, 0:LANES]
    for i in range(1, SUBLANES):
        s = s + x[:, i * LANES:(i + 1) * LANES]
    s = _group_allreduce(s)
    return jnp.concatenate([s] * SUBLANES, axis=1)


def _mod_kernel(c_ref, w_ref, b_ref, o_ref):
    c = c_ref[...]
    s = c * _sigmoid(c)
    o_ref[...] = jnp.dot(s, w_ref[...], precision=HIGHEST, preferred_element_type=F32) + b_ref[...]


def _modulation(c_rows, w_mod, b_mod):
    depth, d, n = w_mod.shape
    rows = c_rows.shape[0]
    tn = 1024
    return pl.pallas_call(
        _mod_kernel,
        grid=(depth, n // tn),
        in_specs=[pl.BlockSpec((rows, d), lambda l, j: (0, 0)),
                  pl.BlockSpec((None, d, tn), lambda l, j: (l, 0, j)),
                  pl.BlockSpec((None, 1, tn), lambda l, j: (l, 0, j))],
        out_specs=pl.BlockSpec((None, rows, tn), lambda l, j: (l, 0, j)),
        out_shape=jax.ShapeDtypeStruct((depth, rows, n), F32),
        compiler_params=_cparams(("parallel", "parallel")),
        name="modulation",
    )(c_rows, w_mod, b_mod.reshape(depth, 1, n))


def _norm_mod(x, g, shift, scale):
    ms = jnp.mean(x * x, axis=-1, keepdims=True)
    h = x * lax.rsqrt(ms + RMS_EPS) * g
    return h * (1.0 + scale) + shift


def _in_kernel(x_ref, g_ref, sh_ref, sc_ref, w_ref, o_ref):
    h = _norm_mod(x_ref[...], g_ref[...], sh_ref[...], sc_ref[...])
    o_ref[...] = jnp.dot(h.astype(BF16), w_ref[...], preferred_element_type=F32)


def _in_proj(x, g, shift, scale, w_bf16):
    b, l, d = x.shape
    n = w_bf16.shape[1]
    tm = min(l, 512)
    tn = 1920
    return pl.pallas_call(
        _in_kernel,
        grid=(n // tn, b, l // tm),
        in_specs=[pl.BlockSpec((None, tm, d), lambda j, bi, i: (bi, i, 0)),
                  pl.BlockSpec((1, d), lambda j, bi, i: (0, 0)),
                  pl.BlockSpec((None, 1, d), lambda j, bi, i: (bi, 0, 0)),
                  pl.BlockSpec((None, 1, d), lambda j, bi, i: (bi, 0, 0)),
                  pl.BlockSpec((d, tn), lambda j, bi, i: (0, j))],
        out_specs=pl.BlockSpec((None, tm, tn), lambda j, bi, i: (bi, i, j)),
        out_shape=jax.ShapeDtypeStruct((b, l, n), F32),
        compiler_params=_cparams(("parallel", "parallel", "parallel")),
        name="in_proj",
    )(x, g, shift, scale, w_bf16)


def _prep_kernel(direction, rkv_ref, rkv_halo_ref, wa_ref, wa_halo_ref, mu_rkv_ref, mu_wa_ref,
                 w0_ref, a0_ref, kk_ref, ka_ref, rk_ref, wl_ref,
                 r_out, k_out, v_out, w_out, kk_out, kka_out, bonus_out):
    i = pl.program_id(1)
    last = pl.num_programs(1) - 1
    tm = rkv_ref.shape[0]
    rows = lax.broadcasted_iota(I32, (tm, 1), 0)

    def shifted(ref, halo_ref):
        z = ref[...]
        if direction == 0:
            halo = halo_ref[SUBLANES - 1:SUBLANES, :] * jnp.where(i == 0, 0.0, 1.0)
            return z, jnp.where(rows == 0, halo, pltpu.roll(z, 1, 0))
        halo = halo_ref[0:1, :] * jnp.where(i == last, 0.0, 1.0)
        return z, jnp.where(rows == tm - 1, halo, pltpu.roll(z, tm - 1, 0))

    z, zs = shifted(rkv_ref, rkv_halo_ref)
    zm = z + mu_rkv_ref[...] * (zs - z)
    r = zm[:, 0:RWKV_WIDTH]
    k = zm[:, RWKV_WIDTH:2 * RWKV_WIDTH]
    v = zm[:, 2 * RWKV_WIDTH:]
    wa, was = shifted(wa_ref, wa_halo_ref)
    wam = wa + mu_wa_ref[...] * (was - wa)
    lane = lax.broadcasted_iota(I32, wam.shape, 1)
    lat = jnp.where(lane < DECAY_LORA, jnp.tanh(wam), wam)
    lo = jnp.dot(lat.astype(BF16), wl_ref[...], preferred_element_type=F32)
    w_log = -_softplus(-(w0_ref[...] + lo[:, 0:RWKV_WIDTH])) - 0.5
    decay = jnp.exp(-jnp.exp(w_log))
    a = _sigmoid(a0_ref[...] + lo[:, RWKV_WIDTH:])
    kk = k * kk_ref[...]
    kk = kk / jnp.maximum(jnp.sqrt(_headsum(kk * kk)), 1e-12)
    k2 = k * (1.0 + (a - 1.0) * ka_ref[...])
    r_out[...] = r
    k_out[...] = k2
    v_out[...] = v
    w_out[...] = decay
    kk_out[...] = kk
    kka_out[...] = kk * a
    bonus_out[...] = _headsum(r * k2 * rk_ref[...]) * v


def _prep(direction, p, mu_rkv, mu_wa, w0, a0, k_k, k_a, r_k, w_lora):
    b, l, _ = p.shape
    tm = min(l, 256)
    nt = l // tm
    hb = tm // SUBLANES
    nh = l // SUBLANES
    if direction == 0:
        halo_idx = lambda i: jnp.maximum(i * hb - 1, 0)
    else:
        halo_idx = lambda i: jnp.minimum((i + 1) * hb, nh - 1)
    rkv_blk = PC_RKV // (3 * RWKV_WIDTH)
    wa_blk = PC_WA // LORA_W + direction
    vec = lambda n: pl.BlockSpec((1, n), lambda bi, i: (0, 0))
    out = jax.ShapeDtypeStruct((b, l, RWKV_WIDTH), F32)
    return pl.pallas_call(
        functools.partial(_prep_kernel, direction),
        grid=(b, nt),
        in_specs=[pl.BlockSpec((None, tm, 3 * RWKV_WIDTH), lambda bi, i: (bi, i, rkv_blk)),
                  pl.BlockSpec((None, SUBLANES, 3 * RWKV_WIDTH), lambda bi, i: (bi, halo_idx(i), rkv_blk)),
                  pl.BlockSpec((None, tm, LORA_W), lambda bi, i: (bi, i, wa_blk)),
                  pl.BlockSpec((None, SUBLANES, LORA_W), lambda bi, i: (bi, halo_idx(i), wa_blk)),
                  vec(3 * RWKV_WIDTH), vec(LORA_W), vec(RWKV_WIDTH), vec(RWKV_WIDTH), vec(RWKV_WIDTH),
                  vec(RWKV_WIDTH), vec(RWKV_WIDTH),
                  pl.BlockSpec((LORA_W, 2 * RWKV_WIDTH), lambda bi, i: (0, 0))],
        out_specs=[pl.BlockSpec((None, tm, RWKV_WIDTH), lambda bi, i: (bi, i, 0))] * 7,
        out_shape=[out] * 7,
        compiler_params=_cparams(("parallel", "parallel")),
        name="rwkv_prep",
    )(p, p, p, p, mu_rkv, mu_wa, w0, a0, k_k, k_a, r_k, w_lora)


def _reduce_scatter_level(cur, lane):
    half = len(cur) // 2
    width = RWKV_HEADS * half
    low = (lane % (2 * width)) < width
    nxt = []
    for j in range(half):
        keep = jnp.where(low, cur[j], cur[j + half])
        other = jnp.where(low, cur[j + half], cur[j])
        if 2 * width == LANES:
            moved = pltpu.roll(other, width, 1)
        else:
            moved = jnp.where(low, pltpu.roll(other, LANES - width, 1), pltpu.roll(other, width, 1))
        nxt.append(keep + moved)
    return nxt


def _scan_kernel(nb, tt, *refs):
    fwd, bwd = refs[0:6], refs[6:12]
    e3_ref, s0_ref, yf_ref, yb_ref, st_ref, s_ref, sa_ref, qr_ref, yr_ref, y1_ref, y2_ref, vb_ref, wkk_ref = refs[12:]
    i = pl.program_id(0)
    lane = lax.broadcasted_iota(I32, (SUBLANES, LANES), 1)
    tile = (SUBLANES, LANES)

    @pl.when(i == 0)
    def _():
        s_ref[...] = s0_ref[...]

    def row(ref, b, t, k2):
        return jnp.broadcast_to(ref[b, pl.ds(t, 1), k2, :], tile)

    def wkk_row(g, k2):
        return jnp.broadcast_to(wkk_ref[g, k2:k2 + 1, :], tile)

    groups = []
    for d, (streams, y_ref) in enumerate(((fwd, yf_ref), (bwd, yb_ref))):
        for b in range(nb):
            groups.append((d * nb + b, d, b, streams, y_ref))

    def advance_y(g, b, y_ref, t_out):
        y_ref[b, t_out] = _reduce_scatter_level([y2_ref[g, j] for j in range(2)], lane)[0]
        lvl2 = _reduce_scatter_level([y1_ref[g, j] for j in range(4)], lane)
        lvl1 = _reduce_scatter_level([yr_ref[g, j] for j in range(JQ)], lane)
        for j in range(2):
            y2_ref[g, j] = lvl2[j]
        for j in range(4):
            y1_ref[g, j] = lvl1[j]

    def head_dot(x, y):
        return _group_allreduce(jnp.broadcast_to(jnp.sum(x * y, axis=0, keepdims=True), tile))

    for g, d, b, streams, _ in groups:
        _, _, v_ref, w_ref, kk_ref, _ = streams
        v = v_ref[b].reshape(tt * SUBLANES, LANES)
        hi = v.astype(BF16)
        lo = (v - hi.astype(F32)).astype(BF16)
        vb = jnp.dot(jnp.concatenate([hi, lo], axis=1), e3_ref[...], preferred_element_type=F32)
        vb_ref[g] = vb.reshape(tt, SUBLANES, JQ * LANES)
        t0, t1 = (0, 1) if d == 0 else (tt - 1, tt - 2)
        wkk_ref[g] = w_ref[b, t0] * kk_ref[b, t1]
        for vq in range(JQ):
            acc_sa = None
            acc_q = None
            for k2 in range(SUBLANES):
                s = s_ref[g, vq, k2]
                psa = s * row(kk_ref, b, t0, k2)
                pq = s * wkk_row(g, k2)
                acc_sa = psa if acc_sa is None else acc_sa + psa
                acc_q = pq if acc_q is None else acc_q + pq
            sa_ref[g, vq] = _group_allreduce(acc_sa)
            qr_ref[g, vq] = acc_q
            yr_ref[g, vq] = jnp.zeros(tile, F32)
        for j in range(4):
            y1_ref[g, j] = jnp.zeros(tile, F32)
        for j in range(2):
            y2_ref[g, j] = jnp.zeros(tile, F32)

    def step(t, carry):
        for g, d, b, streams, y_ref in groups:
            r_ref, k_ref, v_ref, w_ref, kk_ref, ka_ref = streams
            if d == 0:
                tc = t
                t_out = jnp.maximum(t - 3, 0)
                n1 = jnp.minimum(t + 1, tt - 1)
                n2 = jnp.minimum(t + 2, tt - 1)
            else:
                tc = tt - 1 - t
                t_out = jnp.minimum(tc + 3, tt - 1)
                n1 = jnp.maximum(tc - 1, 0)
                n2 = jnp.maximum(tc - 2, 0)
            advance_y(g, b, y_ref, t_out)
            kk1 = kk_ref[b, n1]
            alpha = head_dot(ka_ref[b, tc], kk1)
            beta = head_dot(k_ref[b, tc], kk1)
            q_prev = [_group_allreduce(qr_ref[g, vq]) for vq in range(JQ)]
            wkk_ref[g] = w_ref[b, n1] * kk_ref[b, n2]
            sa = [sa_ref[g, vq] for vq in range(JQ)]
            vbs = [vb_ref[g, tc, :, vq * LANES:(vq + 1) * LANES] for vq in range(JQ)]
            acc_y = [None] * JQ
            acc_q = [None] * JQ
            for k2 in range(SUBLANES):
                w_r = row(w_ref, b, tc, k2)
                ka_r = row(ka_ref, b, tc, k2)
                k_r = row(k_ref, b, tc, k2)
                r_r = row(r_ref, b, tc, k2)
                wkk_r = wkk_row(g, k2)
                for vq in range(JQ):
                    new = s_ref[g, vq, k2] * w_r - sa[vq] * ka_r + vbs[vq] * k_r
                    s_ref[g, vq, k2] = new
                    py = new * r_r
                    pq = new * wkk_r
                    acc_y[vq] = py if acc_y[vq] is None else acc_y[vq] + py
                    acc_q[vq] = pq if acc_q[vq] is None else acc_q[vq] + pq
            for vq in range(JQ):
                yr_ref[g, vq] = acc_y[vq]
                qr_ref[g, vq] = acc_q[vq]
                sa_ref[g, vq] = q_prev[vq] - sa[vq] * alpha + vbs[vq] * beta
        return carry

    lax.fori_loop(0, tt, step, 0)

    for back in (3, 2, 1):
        for g, d, b, _, y_ref in groups:
            advance_y(g, b, y_ref, tt - back if d == 0 else back - 1)

    @pl.when(i == pl.num_programs(0) - 1)
    def _():
        st_ref[...] = s_ref[...]


def _lane_group_expander():
    rows = np.arange(LANES)
    cols = np.arange(JQ * LANES)
    e = ((rows[:, None] // RWKV_HEADS == cols[None, :] // LANES)
         & (rows[:, None] % RWKV_HEADS == cols[None, :] % RWKV_HEADS))
    return np.tile(e.astype(np.float32), (2, 1))


LANE_GROUP_EXPANDER = _lane_group_expander()


def _scan(fwd_streams, bwd_streams, s0, tt=None):
    b, l, _ = fwd_streams[0].shape
    tt = min(l, 64) if tt is None else tt
    assert tt >= 4 and l % tt == 0
    nt = l // tt
    expander = jnp.asarray(LANE_GROUP_EXPANDER, BF16)
    acc_shape = (2 * b, JQ, SUBLANES, LANES)
    tile = lambda a: a.reshape(b, l, SUBLANES, LANES)
    f_spec = pl.BlockSpec((b, tt, SUBLANES, LANES), lambda i: (0, i, 0, 0))
    b_spec = pl.BlockSpec((b, tt, SUBLANES, LANES), lambda i: (0, nt - 1 - i, 0, 0))
    s_shape = (2 * b, JQ, SUBLANES, SUBLANES, LANES)
    s_spec = pl.BlockSpec(s_shape, lambda i: (0, 0, 0, 0, 0))
    y_shape = jax.ShapeDtypeStruct((b, l, SUBLANES, LANES), F32)
    yf, yb, st = pl.pallas_call(
        functools.partial(_scan_kernel, b, tt),
        grid=(nt,),
        in_specs=[f_spec] * 6 + [b_spec] * 6
                 + [pl.BlockSpec(LANE_GROUP_EXPANDER.shape, lambda i: (0, 0)), s_spec],
        out_specs=[f_spec, b_spec, s_spec],
        out_shape=[y_shape, y_shape, jax.ShapeDtypeStruct(s_shape, F32)],
        scratch_shapes=[pltpu.VMEM(s_shape, F32),
                        pltpu.VMEM(acc_shape, F32),
                        pltpu.VMEM(acc_shape, F32),
                        pltpu.VMEM(acc_shape, F32),
                        pltpu.VMEM((2 * b, 4, SUBLANES, LANES), F32),
                        pltpu.VMEM((2 * b, 2, SUBLANES, LANES), F32),
                        pltpu.VMEM((2 * b, tt, SUBLANES, JQ * LANES), F32),
                        pltpu.VMEM((2 * b, SUBLANES, LANES), F32)],
        compiler_params=_cparams(("arbitrary",)),
        name="rwkv_scan",
    )(*[tile(a) for a in fwd_streams], *[tile(a) for a in bwd_streams], expander, s0)
    return yf.reshape(b, l, RWKV_WIDTH), yb.reshape(b, l, RWKV_WIDTH), st


CONV_PAD = 16


def _mix_kernel(seg, yf_ref, yb_ref, bf_ref, bb_ref, gate_ref, glu_ref, glat_ref,
                lng_ref, lnb_ref, g2_ref, wro_ref, cw_ref, cb_ref, clg_ref, clb_ref, wco_ref,
                o_ref, upad_ref, shift_ref):
    tm = yf_ref.shape[0]
    nseg = tm // seg
    y = yf_ref[...] + yb_ref[...]
    mu = _headsum(y) * (1.0 / RWKV_HEAD)
    yc = y - mu
    var = _headsum(yc * yc) * (1.0 / RWKV_HEAD)
    yn = yc * lax.rsqrt(var + GN_EPS) * lng_ref[...] + lnb_ref[...]
    yn = yn + bf_ref[...] + bb_ref[...]
    g = jnp.dot(_sigmoid(glat_ref[...]).astype(BF16), g2_ref[...], preferred_element_type=F32)
    b_r = jnp.dot((yn * g).astype(BF16), wro_ref[...], preferred_element_type=F32)
    u = glu_ref[:, 0:CONV_WIDTH] * _sigmoid(glu_ref[:, CONV_WIDTH:])
    zpad = jnp.zeros((CONV_PAD, CONV_WIDTH), F32)
    for s in range(nseg):
        upad_ref[s, 0:CONV_PAD, :] = zpad
        upad_ref[s, CONV_PAD:CONV_PAD + seg, :] = u[s * seg:(s + 1) * seg, :]
        upad_ref[s, CONV_PAD + seg:, :] = zpad
    first = CONV_PAD - CONV_K // 2
    convs = []
    for s in range(nseg):
        blocks = []
        for c in range(CONV_WIDTH // LANES):
            cols = slice(c * LANES, (c + 1) * LANES)
            acc = None
            for phase in range(SUBLANES):
                taps = [j for j in range(CONV_K) if (first + j) % SUBLANES == phase]
                tiles = [(first + j) // SUBLANES for j in taps]
                span = SUBLANES * max(tiles) + seg
                shift_ref[0:span, :] = upad_ref[s, phase:phase + span, cols]
                for j, q in zip(taps, tiles):
                    term = shift_ref[SUBLANES * q:SUBLANES * q + seg, :] * cw_ref[j:j + 1, cols]
                    acc = term if acc is None else acc + term
            blocks.append(acc)
        convs.append(jnp.concatenate(blocks, axis=1))
    uc = (convs[0] if nseg == 1 else jnp.concatenate(convs, axis=0)) + cb_ref[...]
    m1 = jnp.mean(uc, axis=-1, keepdims=True)
    ucc = uc - m1
    v1 = jnp.mean(ucc * ucc, axis=-1, keepdims=True)
    un = ucc * lax.rsqrt(v1 + LN_EPS) * clg_ref[...] + clb_ref[...]
    b_c = jnp.dot((un * _sigmoid(un)).astype(BF16), wco_ref[...], preferred_element_type=F32)
    g_r = _sigmoid(gate_ref[:, 0:D_MODEL])
    g_c = _sigmoid(gate_ref[:, D_MODEL:])
    o_ref[...] = (g_r * b_r + g_c * b_c).astype(BF16)


def _mix(seg, yf, yb, bonus_f, bonus_b, p, lnx_g, lnx_b, g2, w_ro, conv_w, conv_b, cln_g, cln_b, w_co):
    b, l, _ = yf.shape
    tm = min(l, 256)
    assert tm % seg == 0
    tok = lambda n, blk=0: pl.BlockSpec((None, tm, n), lambda bi, i, blk=blk: (bi, i, blk))
    full = lambda a: pl.BlockSpec(a.shape, lambda bi, i: (0,) * a.ndim)
    params = (lnx_g, lnx_b, g2, w_ro, conv_w, conv_b, cln_g, cln_b, w_co)
    return pl.pallas_call(
        functools.partial(_mix_kernel, seg),
        grid=(b, l // tm),
        in_specs=[tok(RWKV_WIDTH)] * 4 + [tok(2 * D_MODEL, PC_GATE // (2 * D_MODEL)),
                                          tok(2 * CONV_WIDTH, PC_GLU // (2 * CONV_WIDTH)),
                                          tok(GATE_LORA, PC_GLAT // GATE_LORA)]
                 + [full(a) for a in params],
        out_specs=tok(D_MODEL),
        out_shape=jax.ShapeDtypeStruct((b, l, D_MODEL), BF16),
        scratch_shapes=[pltpu.VMEM((tm // seg, seg + 2 * CONV_PAD, CONV_WIDTH), F32),
                        pltpu.VMEM((seg + 2 * CONV_PAD, LANES), F32)],
        compiler_params=_cparams(("parallel", "parallel")),
        name="mixer_branches",
    )(yf, yb, bonus_f, bonus_b, p, p, p, *params)


def _out_kernel(m_ref, w_ref, x_ref, gate_ref, o_ref):
    o_ref[...] = x_ref[...] + gate_ref[...] * jnp.dot(m_ref[...], w_ref[...], preferred_element_type=F32)


def _out_proj(m, w_o, x, gate):
    b, l, d = x.shape
    tm = min(l, 512)
    tok = pl.BlockSpec((None, tm, d), lambda bi, i: (bi, i, 0))
    return pl.pallas_call(
        _out_kernel,
        grid=(b, l // tm),
        in_specs=[tok, pl.BlockSpec((d, d), lambda bi, i: (0, 0)), tok,
                  pl.BlockSpec((None, 1, d), lambda bi, i: (bi, 0, 0))],
        out_specs=tok,
        out_shape=jax.ShapeDtypeStruct((b, l, d), F32),
        compiler_params=_cparams(("parallel", "parallel")),
        name="mixer_out",
    )(m, w_o, x, gate)


def _route_kernel(x_ref, g_ref, sh_ref, sc_ref, wr_ref, hn_ref, aff_ref):
    h = _norm_mod(x_ref[...], g_ref[...], sh_ref[...], sc_ref[...])
    hn_ref[...] = h.astype(BF16)
    logits = lax.dot_general(wr_ref[...], h, (((1,), (1,)), ((), ())), precision=HIGHEST,
                             preferred_element_type=F32)
    m = jnp.max(logits, axis=0, keepdims=True)
    e = jnp.exp(logits - m)
    aff_ref[...] = e / jnp.sum(e, axis=0, keepdims=True)


def _route(x, g, shift, scale, w_router_t):
    b, l, d = x.shape
    tm = min(l, 512)
    row = pl.BlockSpec((None, 1, d), lambda bi, i: (bi, 0, 0))
    return pl.pallas_call(
        _route_kernel,
        grid=(b, l // tm),
        in_specs=[pl.BlockSpec((None, tm, d), lambda bi, i: (bi, i, 0)),
                  pl.BlockSpec((1, d), lambda bi, i: (0, 0)), row, row,
                  pl.BlockSpec((N_EXPERTS, d), lambda bi, i: (0, 0))],
        out_specs=[pl.BlockSpec((None, tm, d), lambda bi, i: (bi, i, 0)),
                   pl.BlockSpec((None, N_EXPERTS, tm), lambda bi, i: (bi, 0, i))],
        out_shape=[jax.ShapeDtypeStruct((b, l, d), BF16), jax.ShapeDtypeStruct((b, N_EXPERTS, l), F32)],
        compiler_params=_cparams(("parallel", "parallel")),
        name="moe_route",
    )(x, g, shift, scale, w_router_t)


SEL_BLK = LANES


def _select_kernel(cap, aff_ref, slot_ref, off_ref):
    ne, l = aff_ref.shape
    nblk = l // SEL_BLK

    def bits():
        return pltpu.bitcast(aff_ref[...], I32)

    def count(mask_i32):
        return jnp.sum(mask_i32, axis=1, keepdims=True)

    def value_step(it, prefix):
        cand = prefix | lax.shift_left(jnp.int32(1), 30 - it)
        cnt = count(jnp.where(bits() >= cand, 1, 0))
        return jnp.where(cnt >= cap, cand, prefix)

    tau = lax.fori_loop(0, 31, value_step, jnp.zeros((ne, 1), I32))
    need = cap - count(jnp.where(bits() > tau, 1, 0))
    tok = lax.broadcasted_iota(I32, (ne, l), 1)
    nbits = int(l).bit_length()

    def index_step(it, tp):
        cand = tp | lax.shift_left(jnp.int32(1), nbits - 1 - it)
        f = count(jnp.where((bits() == tau) & (tok < cand), 1, 0))
        return jnp.where(f < need, cand, tp)

    tlast = lax.fori_loop(0, nbits, index_step, jnp.zeros((ne, 1), I32))
    bt = bits()
    sel = (bt > tau) | ((bt == tau) & (tok <= tlast))
    sel_b = jnp.where(sel, 1.0, 0.0).astype(BF16)
    blk_of = lax.broadcasted_iota(I32, (l, nblk), 0) // SEL_BLK
    blk_id = lax.broadcasted_iota(I32, (l, nblk), 1)
    tot = jnp.dot(sel_b, jnp.where(blk_of == blk_id, 1.0, 0.0).astype(BF16), preferred_element_type=F32)
    ra = lax.broadcasted_iota(I32, (nblk, nblk), 0)
    ca = lax.broadcasted_iota(I32, (nblk, nblk), 1)
    offs = jnp.dot(tot.astype(BF16), jnp.where(ra < ca, 1.0, 0.0).astype(BF16), preferred_element_type=F32)
    off_ref[...] = offs.astype(I32)
    ri = lax.broadcasted_iota(I32, (SEL_BLK, SEL_BLK), 0)
    ci = lax.broadcasted_iota(I32, (SEL_BLK, SEL_BLK), 1)
    strict_upper = jnp.where(ri < ci, 1.0, 0.0).astype(BF16)
    for c in range(nblk):
        sl = slice(c * SEL_BLK, (c + 1) * SEL_BLK)
        excl = jnp.dot(sel_b[:, sl], strict_upper, preferred_element_type=F32)
        pos = (excl + offs[:, c:c + 1]).astype(I32)
        slot_ref[:, sl] = jnp.where(sel[:, sl], pos, -1)


def _select(aff, cap):
    b, ne, l = aff.shape
    nblk = l // SEL_BLK
    return pl.pallas_call(
        functools.partial(_select_kernel, cap),
        grid=(b,),
        in_specs=[pl.BlockSpec((None, ne, l), lambda bi: (bi, 0, 0))],
        out_specs=[pl.BlockSpec((None, ne, l), lambda bi: (bi, 0, 0)),
                   pl.BlockSpec((None, ne, nblk), lambda bi: (bi, 0, 0))],
        out_shape=[jax.ShapeDtypeStruct((b, ne, l), I32), jax.ShapeDtypeStruct((b, ne, nblk), I32)],
        compiler_params=_cparams(("parallel",)),
        name="moe_select",
    )(aff)


GATHER_TS = 256
GATHER_WIN = GATHER_TS + BF16_ROWS
COMBINE_TS = SEL_BLK
COMBINE_WIN = COMBINE_TS + BF16_ROWS


def _expert_kernel(cap, nblk, tb, off_ref, slot_ref, hn_ref, wg_ref, wu_ref, wd_ref, y_ref, xs_ref):
    e, b, n = pl.program_id(0), pl.program_id(1), pl.program_id(2)

    @pl.when(n == 0)
    def _():
        xs_ref[...] = jnp.zeros(xs_ref.shape, xs_ref.dtype)

    base = (b * N_EXPERTS + e) * nblk + n * (tb // SEL_BLK)
    srow = lax.broadcasted_iota(I32, (GATHER_WIN, GATHER_TS), 0)
    for j in range(tb // GATHER_TS):
        p0 = off_ref[base + j * (GATHER_TS // SEL_BLK)]
        start = pl.multiple_of((p0 // BF16_ROWS) * BF16_ROWS, BF16_ROWS)
        rel = slot_ref[:, j * GATHER_TS:(j + 1) * GATHER_TS] - start
        onehot = jnp.where(srow == rel, 1.0, 0.0).astype(BF16)
        rows = jnp.dot(onehot, hn_ref[j * GATHER_TS:(j + 1) * GATHER_TS, :], preferred_element_type=F32)
        xs_ref[pl.ds(start, GATHER_WIN), :] = xs_ref[pl.ds(start, GATHER_WIN), :] + rows.astype(BF16)

    @pl.when(n == pl.num_programs(2) - 1)
    def _():
        xs = xs_ref[0:cap, :]
        gh = jnp.dot(xs, wg_ref[...], preferred_element_type=F32)
        uh = jnp.dot(xs, wu_ref[...], preferred_element_type=F32)
        act = (gh * _sigmoid(gh)) * uh
        y_ref[0:cap, :] = jnp.dot(act.astype(BF16), wd_ref[...], preferred_element_type=F32).astype(BF16)
        y_ref[cap:, :] = jnp.zeros((y_ref.shape[0] - cap, y_ref.shape[1]), BF16)


def _experts(off_flat, slot4, hn, wg, wu, wd, cap):
    b, l, d = hn.shape
    nblk = l // SEL_BLK
    tb = min(l, 1024)
    yr = cap + COMBINE_WIN
    grid_spec = pltpu.PrefetchScalarGridSpec(
        num_scalar_prefetch=1,
        grid=(N_EXPERTS, b, l // tb),
        in_specs=[pl.BlockSpec((None, None, 1, tb), lambda e, bi, n, off: (bi, e, 0, n)),
                  pl.BlockSpec((None, tb, d), lambda e, bi, n, off: (bi, n, 0)),
                  pl.BlockSpec((None, d, EXPERT_FF), lambda e, bi, n, off: (e, 0, 0)),
                  pl.BlockSpec((None, d, EXPERT_FF), lambda e, bi, n, off: (e, 0, 0)),
                  pl.BlockSpec((None, EXPERT_FF, d), lambda e, bi, n, off: (e, 0, 0))],
        out_specs=pl.BlockSpec((None, None, yr, d), lambda e, bi, n, off: (bi, e, 0, 0)),
        scratch_shapes=[pltpu.VMEM((cap + GATHER_WIN, d), BF16)])
    return pl.pallas_call(
        functools.partial(_expert_kernel, cap, nblk, tb),
        grid_spec=grid_spec,
        out_shape=jax.ShapeDtypeStruct((b, N_EXPERTS, yr, d), BF16),
        compiler_params=_cparams(("parallel", "parallel", "arbitrary")),
        name="moe_experts",
    )(off_flat, slot4, hn, wg, wu, wd)


def _combine_kernel(nblk, tb, off_ref, slot_ref, aff_ref, y_ref, x_ref, gate_ref, o_ref, acc_ref):
    b, n, e = pl.program_id(0), pl.program_id(1), pl.program_id(2)

    @pl.when(e == 0)
    def _():
        acc_ref[...] = jnp.zeros(acc_ref.shape, F32)

    base = (b * N_EXPERTS + e) * nblk + n * (tb // SEL_BLK)
    srow = lax.broadcasted_iota(I32, (COMBINE_WIN, COMBINE_TS), 0)
    for j in range(tb // COMBINE_TS):
        sl = slice(j * COMBINE_TS, (j + 1) * COMBINE_TS)
        p0 = off_ref[base + j]
        start = pl.multiple_of((p0 // BF16_ROWS) * BF16_ROWS, BF16_ROWS)
        rel = slot_ref[:, sl] - start
        wt = jnp.where(srow == rel, aff_ref[:, sl], 0.0).astype(BF16)
        contrib = lax.dot_general(wt, y_ref[pl.ds(start, COMBINE_WIN), :], (((0,), (0,)), ((), ())),
                                  preferred_element_type=F32)
        acc_ref[sl, :] = acc_ref[sl, :] + contrib

    @pl.when(e == pl.num_programs(2) - 1)
    def _():
        o_ref[...] = x_ref[...] + gate_ref[...] * acc_ref[...]


def _combine(off_flat, slot4, aff4, y, x, gate):
    b, l, d = x.shape
    nblk = l // SEL_BLK
    tb = min(l, 1024)
    yr = y.shape[2]
    tok1 = pl.BlockSpec((None, None, 1, tb), lambda bi, n, e, off: (bi, e, 0, n))
    tokd = pl.BlockSpec((None, tb, d), lambda bi, n, e, off: (bi, n, 0))
    grid_spec = pltpu.PrefetchScalarGridSpec(
        num_scalar_prefetch=1,
        grid=(b, l // tb, N_EXPERTS),
        in_specs=[tok1, tok1,
                  pl.BlockSpec((None, None, yr, d), lambda bi, n, e, off: (bi, e, 0, 0)),
                  tokd, pl.BlockSpec((None, 1, d), lambda bi, n, e, off: (bi, 0, 0))],
        out_specs=tokd,
        scratch_shapes=[pltpu.VMEM((tb, d), F32)])
    return pl.pallas_call(
        functools.partial(_combine_kernel, nblk, tb),
        grid_spec=grid_spec,
        out_shape=jax.ShapeDtypeStruct((b, l, d), F32),
        compiler_params=_cparams(("parallel", "parallel", "arbitrary")),
        name="moe_combine",
    )(off_flat, slot4, aff4, y, x, gate)


def _final_kernel(x_ref, g_ref, o_ref):
    x = x_ref[...]
    ms = jnp.mean(x * x, axis=-1, keepdims=True)
    o_ref[...] = x * lax.rsqrt(ms + RMS_EPS) * g_ref[...]


def _final_norm(x, g):
    b, l, d = x.shape
    tm = min(l, 512)
    tok = pl.BlockSpec((None, tm, d), lambda bi, i: (bi, i, 0))
    return pl.pallas_call(
        _final_kernel,
        grid=(b, l // tm),
        in_specs=[tok, pl.BlockSpec((1, d), lambda bi, i: (0, 0))],
        out_specs=tok,
        out_shape=jax.ShapeDtypeStruct((b, l, d), F32),
        compiler_params=_cparams(("parallel", "parallel")),
        name="final_norm",
    )(x, g)


def _rwkv_streams(p, lw):
    outs = []
    for d in range(2):
        outs.append(_prep(d, p, lw['mu_rkv'][d], lw['mu_wa'][d], lw['w0'][d], lw['a0'][d], lw['k_k'][d],
                          lw['k_a'][d], lw['r_k'], lw['w_lora'][d]))
    return outs


def _mixer(x, mods, lw, s0, seg, tail=True):
    p = _in_proj(x, lw['norm1_g'], mods[0], mods[1], lw['w_in'])
    sf, sb = _rwkv_streams(p, lw)
    yf, yb, st = _scan(sf[:6], sb[:6], s0)
    if not tail:
        return None, st
    m = _mix(seg, yf, yb, sf[6], sb[6], p, lw['lnx_g'], lw['lnx_b'], lw['g2'], lw['w_rwkv_out'],
             lw['conv_w'], lw['conv_b'], lw['conv_ln_g'], lw['conv_ln_b'], lw['w_conv_out'])
    return _out_proj(m, lw['w_o'], x, mods[2]), st


def _moe(x, mods, lw):
    b, l, d = x.shape
    cap = EC_CAPACITY * l // N_EXPERTS
    hn, aff = _route(x, lw['norm2_g'], mods[3], mods[4], lw['w_router_t'])
    slot, off = _select(aff, cap)
    slot4 = slot.reshape(b, N_EXPERTS, 1, l)
    aff4 = aff.reshape(b, N_EXPERTS, 1, l)
    off_flat = off.reshape(-1)
    y = _experts(off_flat, slot4, hn, lw['w_exp_gate'], lw['w_exp_up'], lw['w_exp_down'], cap)
    return _combine(off_flat, slot4, aff4, y, x, mods[5])


def _perm_last(a):
    lead = a.shape[:-1]
    a = a.reshape(lead + (RWKV_HEADS, SUBLANES, JQ))
    return jnp.moveaxis(a, -3, -1).reshape(lead + (RWKV_WIDTH,))


def _layer_weights(l, w):
    row = lambda a: a.reshape(1, -1)
    mu = w['tok_mu'][l]
    mu_rkv = [row(jnp.concatenate([_perm_last(mu[d, i * RWKV_WIDTH:(i + 1) * RWKV_WIDTH]) for i in range(3)]))
              for d in range(2)]
    mu_wa = [row(mu[d, 3 * RWKV_WIDTH:]) for d in range(2)]
    zeros = jnp.zeros((DECAY_LORA, RWKV_WIDTH), F32)
    w_lora = [jnp.concatenate([jnp.concatenate([_perm_last(w['w2'][l, d]), zeros], axis=1),
                               jnp.concatenate([zeros, _perm_last(w['a2'][l, d])], axis=1)], axis=0).astype(BF16)
              for d in range(2)]
    pv = lambda a: [row(_perm_last(a[l, d])) for d in range(2)]
    wi = w['w_in'][l]
    w_in = jnp.concatenate(
        [wi[:, OFF_GATE:], wi[:, OFF_GLU:OFF_GATE]]
        + [_perm_last(wi[:, i * RWKV_WIDTH:(i + 1) * RWKV_WIDTH]) for i in range(3)]
        + [wi[:, OFF_WA_F:OFF_GLU]], axis=1).astype(BF16)
    return dict(
        norm1_g=row(w['norm1_g'][l]),
        w_in=w_in,
        mu_rkv=mu_rkv, mu_wa=mu_wa, w_lora=w_lora,
        w0=pv(w['w0']), a0=pv(w['a0']), k_k=pv(w['k_k']), k_a=pv(w['k_a']),
        r_k=row(_perm_last(w['r_k'][l])), lnx_g=row(_perm_last(w['lnx_g'][l])),
        lnx_b=row(_perm_last(w['lnx_b'][l])),
        g2=_perm_last(w['g2'][l]).astype(BF16),
        w_rwkv_out=_perm_last(w['w_rwkv_out'][l].T).T.astype(BF16),
        conv_w=w['conv_w'][l], conv_b=row(w['conv_b'][l]), conv_ln_g=row(w['conv_ln_g'][l]),
        conv_ln_b=row(w['conv_ln_b'][l]), w_conv_out=w['w_conv_out'][l].astype(BF16),
        w_o=w['w_o'][l].astype(BF16), norm2_g=row(w['norm2_g'][l]),
        w_router_t=w['w_router'][l].T,
        w_exp_gate=w['w_exp_gate'][l].astype(BF16), w_exp_up=w['w_exp_up'][l].astype(BF16),
        w_exp_down=w['w_exp_down'][l].astype(BF16))


def kernel(x, c, ctx, c_ctx, w_mod, b_mod, norm1_g, w_in, tok_mu, w0, w2, a0, a2, k_k, k_a, r_k, g2, lnx_g,
           lnx_b, w_rwkv_out, conv_w, conv_b, conv_ln_g, conv_ln_b, w_conv_out, w_o, norm2_g, w_router,
           w_exp_gate, w_exp_up, w_exp_down, final_g):
    w = dict(norm1_g=norm1_g, w_in=w_in, tok_mu=tok_mu, w0=w0, w2=w2, a0=a0, a2=a2, k_k=k_k, k_a=k_a, r_k=r_k,
             g2=g2, lnx_g=lnx_g, lnx_b=lnx_b, w_rwkv_out=w_rwkv_out, conv_w=conv_w, conv_b=conv_b,
             conv_ln_g=conv_ln_g, conv_ln_b=conv_ln_b, w_conv_out=w_conv_out, w_o=w_o, norm2_g=norm2_g,
             w_router=w_router, w_exp_gate=w_exp_gate, w_exp_up=w_exp_up, w_exp_down=w_exp_down)
    batch = x.shape[0]
    depth = w_mod.shape[0]
    ctx_len = ctx.shape[1]
    pad_rows = (-(batch + 1)) % SUBLANES
    c_rows = jnp.concatenate([c, c_ctx[None, :], jnp.zeros((pad_rows, D_MODEL), F32)], axis=0)
    mod = _modulation(c_rows, w_mod, b_mod)
    zero_state = jnp.zeros((2 * batch, JQ, SUBLANES, SUBLANES, LANES), F32)
    for l in range(depth):
        lw = _layer_weights(l, w)
        m_lat = [mod[l, 0:batch, i * D_MODEL:(i + 1) * D_MODEL][:, None, :] for i in range(N_MOD)]
        m_ctx = [jnp.broadcast_to(mod[l, batch:batch + 1, i * D_MODEL:(i + 1) * D_MODEL][:, None, :],
                                  (batch, 1, D_MODEL)) for i in range(N_MOD)]
        last = l == depth - 1
        ctx_mid, state = _mixer(ctx, m_ctx, lw, zero_state, ctx_len, tail=not last)
        x, _ = _mixer(x, m_lat, lw, state, GRID_W)
        x = _moe(x, m_lat, lw)
        if not last:
            ctx = _moe(ctx_mid, m_ctx, lw)
    return _final_norm(x, final_g.reshape(1, -1))
```
